```python
import jax, jax.numpy as jnp
from jax import lax
import numpy as np

D_MODEL = 2048
BATCH = 8
SEQ = 4096
DEPTH = 2
DEC_BATCH = 8
DEC_SEQ = 16
PAST_LEN = 2048

CHUNK = 64
N_MIXERS = 2
N_FOX_LAYERS = (DEPTH + 1) // 2
N_LRU_LAYERS = DEPTH // 2
FOX_HEADS = 16
FOX_HEAD_DIM = D_MODEL // FOX_HEADS
Q_BLOCK = 128
FORGET_BIAS = 3.0
LRU_WIDTH = D_MODEL
LRU_BLOCKS = 8
LRU_BLOCK_W = LRU_WIDTH // LRU_BLOCKS
CONV_W = 4
LRU_C = 8.0
N_MEM = 256
MEM_HEADS = 4
MEM_HEAD_DIM = 128
MEM_WIDTH = MEM_HEADS * MEM_HEAD_DIM
D_FF = -(-8 * D_MODEL // (3 * 256)) * 256
EPS = 1e-6

kernel_name = 'fox_rglru_hybrid_stream_step'


def rmsnorm(x, g):
    xf = x.astype(jnp.float32)
    y = xf * lax.rsqrt(jnp.mean(xf * xf, axis=-1, keepdims=True) + EPS) * g.astype(jnp.float32)
    return y.astype(x.dtype)


def fox_project(h, w_in, b_f):
    B, T, _ = h.shape
    proj = h @ w_in
    q = proj[..., :D_MODEL].reshape(B, T, FOX_HEADS, FOX_HEAD_DIM)
    k = proj[..., D_MODEL:2 * D_MODEL].reshape(B, T, FOX_HEADS, FOX_HEAD_DIM)
    v = proj[..., 2 * D_MODEL:3 * D_MODEL].reshape(B, T, FOX_HEADS, FOX_HEAD_DIM)
    logf = jax.nn.log_sigmoid(proj[..., 3 * D_MODEL:].astype(jnp.float32) + b_f.astype(jnp.float32))
    return q, k, v, logf


def fox_attend(q, cq, qpos, k, v, ck, kpos):
    s = jnp.einsum('bqhd,bkhd->bhqk', q, k).astype(jnp.float32) * (FOX_HEAD_DIM ** -0.5)
    bias = jnp.swapaxes(cq, 1, 2)[:, :, :, None] - jnp.swapaxes(ck, 1, 2)[:, :, None, :]
    mask = kpos[None, :] <= qpos[:, None]
    p = jax.nn.softmax(jnp.where(mask, s + bias, -jnp.inf), axis=-1)
    return jnp.einsum('bhqk,bkhd->bqhd', p.astype(v.dtype), v)


def fox_prompt_attend(q, k, v, logf):
    B, S = q.shape[:2]
    c = jnp.cumsum(logf, axis=1)
    kpos = jnp.arange(S)

    def block(start):
        qb = lax.dynamic_slice_in_dim(q, start, Q_BLOCK, axis=1)
        cb = lax.dynamic_slice_in_dim(c, start, Q_BLOCK, axis=1)
        return fox_attend(qb, cb, start + jnp.arange(Q_BLOCK), k, v, c, kpos)

    out = lax.map(block, jnp.arange(S // Q_BLOCK) * Q_BLOCK)
    return jnp.moveaxis(out, 0, 1).reshape(B, S, D_MODEL)


def fox_sample_attend(q, k, v, logf, cache_k, cache_v, cache_logf):
    B, T = q.shape[:2]
    P = cache_k.shape[1]
    k_all = jnp.concatenate([cache_k.astype(k.dtype), k], axis=1)
    v_all = jnp.concatenate([cache_v.astype(v.dtype), v], axis=1)
    c = jnp.cumsum(jnp.concatenate([cache_logf.astype(jnp.float32), logf], axis=1), axis=1)
    out = fox_attend(q, c[:, P:], P + jnp.arange(T), k_all, v_all, c, jnp.arange(P + T))
    return out.reshape(B, T, D_MODEL)


def rglru_block(h, w_in, conv_w, conv_b, w_ga, b_a, w_gx, b_x, lam, w_out, h0, conv_buf):
    B, T, _ = h.shape
    proj = h @ w_in
    gate = jax.nn.gelu(proj[..., :LRU_WIDTH])
    u = proj[..., LRU_WIDTH:]
    padded = jnp.concatenate([conv_buf.astype(u.dtype), u], axis=1)
    uc = conv_b + sum(padded[:, k:k + T] * conv_w[k] for k in range(CONV_W))
    new_buf = padded[:, padded.shape[1] - (CONV_W - 1):]
    ub = uc.reshape(B, T, LRU_BLOCKS, LRU_BLOCK_W)
    r = jax.nn.sigmoid((jnp.einsum('btnc,ncd->btnd', ub, w_ga).reshape(B, T, LRU_WIDTH) + b_a).astype(jnp.float32))
    ig = jax.nn.sigmoid((jnp.einsum('btnc,ncd->btnd', ub, w_gx).reshape(B, T, LRU_WIDTH) + b_x).astype(jnp.float32))
    log_a = -LRU_C * jax.nn.softplus(-lam.astype(jnp.float32)) * r
    a = jnp.exp(log_a)
    b = jnp.sqrt(-jnp.expm1(2.0 * log_a)) * (ig * uc.astype(jnp.float32))

    def step(hc, inp):
        a_t, b_t = inp
        hc = a_t * hc + b_t
        return hc, hc

    h_last, hs = lax.scan(step, h0.astype(jnp.float32), (jnp.swapaxes(a, 0, 1), jnp.swapaxes(b, 0, 1)))
    y = jnp.swapaxes(hs, 0, 1).astype(h.dtype) * gate
    return y @ w_out, h_last, new_buf


def mem_kv(mem, g, w_kv):
    B, N, _ = mem.shape
    kv = rmsnorm(mem, g) @ w_kv
    k = kv[..., :MEM_WIDTH].reshape(B, N, MEM_HEADS, MEM_HEAD_DIM)
    v = kv[..., MEM_WIDTH:].reshape(B, N, MEM_HEADS, MEM_HEAD_DIM)
    return k, v


def mem_xattn(h, w_q, mk, mv, w_o):
    B, T, _ = h.shape
    q = (h @ w_q).reshape(B, T, MEM_HEADS, MEM_HEAD_DIM)
    s = jnp.einsum('bqhd,bmhd->bhqm', q, mk.astype(q.dtype)).astype(jnp.float32) * (MEM_HEAD_DIM ** -0.5)
    p = jax.nn.softmax(s, axis=-1)
    o = jnp.einsum('bhqm,bmhd->bqhd', p.astype(h.dtype), mv.astype(h.dtype))
    return o.reshape(B, T, MEM_WIDTH) @ w_o


def swiglu(h, w_in, w_out):
    gu = h @ w_in
    return (jax.nn.silu(gu[..., :D_FF]) * gu[..., D_FF:]) @ w_out


def setup_inputs(seed: int = 0) -> dict:
    key = jax.random.key(seed)
    ks = iter(jax.random.split(key, 40))
    f32 = jnp.float32

    def nrm(shape, scale):
        return jax.random.normal(next(ks), shape, f32) * scale

    def gain(shape):
        return 1.0 + 0.05 * jax.random.normal(next(ks), shape, f32)

    a0 = jax.random.uniform(next(ks), (N_LRU_LAYERS, LRU_WIDTH), f32, 0.9, 0.999)
    d = {
        'x_prompt': nrm((BATCH, SEQ, D_MODEL), 1.0),
        'x_sample': nrm((DEC_BATCH, DEC_SEQ, D_MODEL), 1.0),
        'mem_prompt': nrm((BATCH, N_MEM, D_MODEL), 1.0),
        'cache_fox_k': nrm((N_FOX_LAYERS, DEC_BATCH, PAST_LEN, FOX_HEADS, FOX_HEAD_DIM), 1.0),
        'cache_fox_v': nrm((N_FOX_LAYERS, DEC_BATCH, PAST_LEN, FOX_HEADS, FOX_HEAD_DIM), 1.0),
        'cache_fox_logf': jax.nn.log_sigmoid(FORGET_BIAS + nrm((N_FOX_LAYERS, DEC_BATCH, PAST_LEN, FOX_HEADS), 0.5)),
        'cache_mem_k': nrm((DEPTH, DEC_BATCH, N_MEM, MEM_HEADS, MEM_HEAD_DIM), 1.0),
        'cache_mem_v': nrm((DEPTH, DEC_BATCH, N_MEM, MEM_HEADS, MEM_HEAD_DIM), 1.0),
        'state_lru_h': nrm((N_LRU_LAYERS, DEC_BATCH, LRU_WIDTH), 0.5),
        'state_lru_conv': nrm((N_LRU_LAYERS, DEC_BATCH, CONV_W - 1, LRU_WIDTH), 1.0),
        'norm_mix': gain((DEPTH, D_MODEL)),
        'norm_mem': gain((DEPTH, D_MODEL)),
        'norm_xattn': gain((DEPTH, D_MODEL)),
        'norm_ffn': gain((DEPTH, D_MODEL)),
        'norm_final': gain((D_MODEL,)),
        'fox_w_in': nrm((N_FOX_LAYERS, D_MODEL, 3 * D_MODEL + FOX_HEADS), D_MODEL ** -0.5),
        'fox_b_f': FORGET_BIAS + nrm((N_FOX_LAYERS, FOX_HEADS), 0.1),
        'fox_w_out': nrm((N_FOX_LAYERS, D_MODEL, D_MODEL), D_MODEL ** -0.5),
        'lru_w_in': nrm((N_LRU_LAYERS, D_MODEL, 2 * LRU_WIDTH), D_MODEL ** -0.5),
        'lru_conv_w': nrm((N_LRU_LAYERS, CONV_W, LRU_WIDTH), CONV_W ** -0.5),
        'lru_conv_b': nrm((N_LRU_LAYERS, LRU_WIDTH), 0.01),
        'lru_w_ga': nrm((N_LRU_LAYERS, LRU_BLOCKS, LRU_BLOCK_W, LRU_BLOCK_W), LRU_BLOCK_W ** -0.5),
        'lru_b_a': nrm((N_LRU_LAYERS, LRU_WIDTH), 0.01),
        'lru_w_gx': nrm((N_LRU_LAYERS, LRU_BLOCKS, LRU_BLOCK_W, LRU_BLOCK_W), LRU_BLOCK_W ** -0.5),
        'lru_b_x': nrm((N_LRU_LAYERS, LRU_WIDTH), 0.01),
        'lru_lambda': jnp.log(a0) - jnp.log1p(-a0),
        'lru_w_out': nrm((N_LRU_LAYERS, LRU_WIDTH, D_MODEL), LRU_WIDTH ** -0.5),
        'xattn_w_q': nrm((DEPTH, D_MODEL, MEM_WIDTH), D_MODEL ** -0.5),
        'xattn_w_kv': nrm((DEPTH, D_MODEL, 2 * MEM_WIDTH), D_MODEL ** -0.5),
        'xattn_w_o': nrm((DEPTH, MEM_WIDTH, D_MODEL), MEM_WIDTH ** -0.5),
        'ffn_w_in': nrm((DEPTH, D_MODEL, 2 * D_FF), D_MODEL ** -0.5),
        'ffn_w_out': nrm((DEPTH, D_FF, D_MODEL), D_FF ** -0.5),
    }
    return d


def reference(x_prompt, x_sample, mem_prompt, cache_fox_k, cache_fox_v, cache_fox_logf,
              cache_mem_k, cache_mem_v, state_lru_h, state_lru_conv,
              norm_mix, norm_mem, norm_xattn, norm_ffn, norm_final,
              fox_w_in, fox_b_f, fox_w_out,
              lru_w_in, lru_conv_w, lru_conv_b, lru_w_ga, lru_b_a, lru_w_gx, lru_b_x, lru_lambda, lru_w_out,
              xattn_w_q, xattn_w_kv, xattn_w_o, ffn_w_in, ffn_w_out):
    xp, xs = x_prompt, x_sample
    pk, pv, plf, pmk, pmv, ph, pc = [], [], [], [], [], [], []
    sk, sv, slf, sh, sc = [], [], [], [], []
    for i in range(DEPTH):
        j = i // N_MIXERS
        hp = rmsnorm(xp, norm_mix[i])
        hs = rmsnorm(xs, norm_mix[i])
        if i % N_MIXERS == 0:
            qp, kp, vp, lfp = fox_project(hp, fox_w_in[j], fox_b_f[j])
            xp = xp + fox_prompt_attend(qp, kp, vp, lfp) @ fox_w_out[j]
            qs, ks_, vs, lfs = fox_project(hs, fox_w_in[j], fox_b_f[j])
            xs = xs + fox_sample_attend(qs, ks_, vs, lfs, cache_fox_k[j], cache_fox_v[j], cache_fox_logf[j]) @ fox_w_out[j]
            pk.append(kp); pv.append(vp); plf.append(lfp)
            sk.append(ks_); sv.append(vs); slf.append(lfs)
        else:
            h0 = jnp.zeros((xp.shape[0], LRU_WIDTH), xp.dtype)
            buf0 = jnp.zeros((xp.shape[0], CONV_W - 1, LRU_WIDTH), xp.dtype)
            op, hlp, bp = rglru_block(hp, lru_w_in[j], lru_conv_w[j], lru_conv_b[j], lru_w_ga[j], lru_b_a[j],
                                      lru_w_gx[j], lru_b_x[j], lru_lambda[j], lru_w_out[j], h0, buf0)
            os_, hls, bs = rglru_block(hs, lru_w_in[j], lru_conv_w[j], lru_conv_b[j], lru_w_ga[j], lru_b_a[j],
                                       lru_w_gx[j], lru_b_x[j], lru_lambda[j], lru_w_out[j], state_lru_h[j], state_lru_conv[j])
            xp = xp + op
            xs = xs + os_
            ph.append(hlp); pc.append(bp); sh.append(hls); sc.append(bs)
        mk, mv = mem_kv(mem_prompt, norm_mem[i], xattn_w_kv[i])
        pmk.append(mk); pmv.append(mv)
        xp = xp + mem_xattn(rmsnorm(xp, norm_xattn[i]), xattn_w_q[i], mk, mv, xattn_w_o[i])
        xs = xs + mem_xattn(rmsnorm(xs, norm_xattn[i]), xattn_w_q[i], cache_mem_k[i], cache_mem_v[i], xattn_w_o[i])
        xp = xp + swiglu(rmsnorm(xp, norm_ffn[i]), ffn_w_in[i], ffn_w_out[i])
        xs = xs + swiglu(rmsnorm(xs, norm_ffn[i]), ffn_w_in[i], ffn_w_out[i])
    y_prompt = rmsnorm(xp, norm_final)
    y_sample = rmsnorm(xs, norm_final)
    return (y_prompt, y_sample,
            jnp.stack(pk), jnp.stack(pv), jnp.stack(plf), jnp.stack(pmk), jnp.stack(pmv),
            jnp.stack(ph), jnp.stack(pc),
            jnp.stack(sk), jnp.stack(sv), jnp.stack(slf), jnp.stack(sh), jnp.stack(sc))
```

```python
import functools

import jax
import jax.numpy as jnp
from jax import lax
from jax.experimental import pallas as pl
from jax.experimental.pallas import tpu as pltpu

F32 = jnp.float32
BF16 = jnp.bfloat16

EPS = 1e-6
LRU_C = 8.0
LANES = 128
SUBLANES = 8
V7X_VMEM_BUDGET = 56 * 1024 * 1024


def _pick_tile(n, pref, align):
    if n <= pref:
        return n
    t = (pref // align) * align
    while t >= align:
        if n % t == 0:
            return t
        t -= align
    raise ValueError(f"no {align}-aligned tile of {n} below {pref}")


def _params(semantics, vmem_estimate):
    limit = int(min(V7X_VMEM_BUDGET, max(2 * vmem_estimate, 16 * 1024 * 1024)))
    return pltpu.CompilerParams(dimension_semantics=semantics, vmem_limit_bytes=limit)


def _rms(xf, g):
    return xf * lax.rsqrt(jnp.mean(xf * xf, axis=-1, keepdims=True) + EPS) * g


def _identity(x):
    return x


def _softplus(x):
    return jnp.maximum(x, 0.0) + jnp.log1p(jnp.exp(-jnp.abs(x)))


def _log_sigmoid(x):
    return -_softplus(-x)


def _norm_proj_kernel(*refs, acts, extra_act):
    n_sec = len(acts)
    x_ref, g_ref = refs[0], refs[1]
    w_refs = refs[2:2 + n_sec]
    pos = 2 + n_sec
    if extra_act is not None:
        we_ref, be_ref = refs[pos], refs[pos + 1]
        pos += 2
    o_refs = refs[pos:pos + n_sec]
    pos += n_sec
    if extra_act is not None:
        e_ref = refs[pos]
        pos += 1
    h_ref = refs[pos]

    @pl.when(pl.program_id(1) == 0)
    def _():
        h = _rms(x_ref[...], g_ref[...]).astype(BF16)
        h_ref[...] = h
        if extra_act is not None:
            z = jnp.dot(h, we_ref[...], preferred_element_type=F32) + be_ref[...]
            e_ref[...] = extra_act(z)[:, :e_ref.shape[1]]

    h = h_ref[...]
    for w_ref, o_ref, act in zip(w_refs, o_refs, acts):
        o_ref[...] = act(jnp.dot(h, w_ref[...], preferred_element_type=F32)).astype(o_ref.dtype)


def _norm_proj(x, g, w, acts, dtypes, *, name, extra=None, tm_pref=1024, tn_pref=256):
    m, d = x.shape
    n_sec = len(acts)
    ns = w.shape[1] // n_sec
    tm = _pick_tile(m, tm_pref, SUBLANES)
    tn = _pick_tile(ns, tn_pref, LANES)
    tiles = ns // tn
    in_specs = [pl.BlockSpec((tm, d), lambda i, j: (i, 0)),
                pl.BlockSpec((1, d), lambda i, j: (0, 0))]
    in_specs += [pl.BlockSpec((d, tn), lambda i, j, s=s: (0, s * tiles + j)) for s in range(n_sec)]
    args = [x, g.reshape(1, d)] + [w] * n_sec
    out_specs = [pl.BlockSpec((tm, tn), lambda i, j: (i, j)) for _ in range(n_sec)]
    out_shape = [jax.ShapeDtypeStruct((m, ns), dt) for dt in dtypes]
    extra_act = None
    if extra is not None:
        w_e, b_e, n_e, extra_act = extra
        in_specs += [pl.BlockSpec(w_e.shape, lambda i, j: (0, 0)), pl.BlockSpec(b_e.shape, lambda i, j: (0, 0))]
        args += [w_e, b_e]
        out_specs.append(pl.BlockSpec((tm, n_e), lambda i, j: (i, 0)))
        out_shape.append(jax.ShapeDtypeStruct((m, n_e), F32))
    est = 2 * tm * d * 4 + tm * d * 2 + n_sec * 2 * (d * tn * 2 + tm * tn * 4) + 4 * tm * tn * 4
    return pl.pallas_call(
        functools.partial(_norm_proj_kernel, acts=tuple(acts), extra_act=extra_act),
        grid=(m // tm, tiles),
        in_specs=in_specs, out_specs=out_specs, out_shape=out_shape,
        scratch_shapes=[pltpu.VMEM((tm, d), BF16)],
        compiler_params=_params(("parallel", "arbitrary"), est),
        name=name,
    )(*args)


def _matmul_res_kernel(a_ref, w_ref, r_ref, o_ref, *, tn):
    a = a_ref[...]
    for n0 in range(0, o_ref.shape[1], tn):
        o_ref[:, n0:n0 + tn] = r_ref[:, n0:n0 + tn] + jnp.dot(a, w_ref[:, n0:n0 + tn], preferred_element_type=F32)


def _matmul_res(a, w, res, *, name, tm_pref=512, tn_pref=512):
    m, k = a.shape
    n = w.shape[1]
    tm = _pick_tile(m, tm_pref, SUBLANES)
    tn = _pick_tile(n, tn_pref, LANES)
    est = 2 * (tm * k * 2 + k * n * 2 + 2 * tm * n * 4) + 2 * tm * tn * 4
    return pl.pallas_call(
        functools.partial(_matmul_res_kernel, tn=tn),
        grid=(m // tm,),
        in_specs=[pl.BlockSpec((tm, k), lambda i: (i, 0)),
                  pl.BlockSpec((k, n), lambda i: (0, 0)),
                  pl.BlockSpec((tm, n), lambda i: (i, 0))],
        out_specs=pl.BlockSpec((tm, n), lambda i: (i, 0)),
        out_shape=jax.ShapeDtypeStruct((m, n), F32),
        compiler_params=_params(("parallel",), est),
        name=name,
    )(a, w, res)


def _cumsum_kernel(x_ref, o_ref, carry_ref):
    @pl.when(pl.program_id(0) == 0)
    def _():
        carry_ref[...] = jnp.zeros_like(carry_ref)

    x = x_ref[...]
    n = x.shape[1]
    upper = (lax.broadcasted_iota(jnp.int32, (n, n), 0) <= lax.broadcasted_iota(jnp.int32, (n, n), 1)).astype(BF16)
    hi = x.astype(BF16)
    rest = x - hi.astype(F32)
    mid = rest.astype(BF16)
    lo = (rest - mid.astype(F32)).astype(BF16)
    local = (jnp.dot(hi, upper, preferred_element_type=F32) + jnp.dot(mid, upper, preferred_element_type=F32)
             + jnp.dot(lo, upper, preferred_element_type=F32))
    out = local + carry_ref[...]
    o_ref[...] = out
    carry_ref[...] = jnp.broadcast_to(out[:, n - 1:n], out.shape)


def _cumsum_lanes(x, *, name):
    r, l = x.shape
    return pl.pallas_call(
        _cumsum_kernel,
        grid=(l // LANES,),
        in_specs=[pl.BlockSpec((r, LANES), lambda t: (0, t))],
        out_specs=pl.BlockSpec((r, LANES), lambda t: (0, t)),
        out_shape=jax.ShapeDtypeStruct((r, l), F32),
        scratch_shapes=[pltpu.VMEM((r, LANES), F32)],
        compiler_params=_params(("arbitrary",), 8 * r * LANES * 4),
        name=name,
    )(x)


def _fox_attn_kernel(q_ref, k_ref, v_ref, ck_ref, o_ref, kb_ref, vb_ref, *, tq, scale):
    s_len = q_ref.shape[1]
    kb_ref[...] = k_ref[0].astype(BF16)
    vb_ref[...] = v_ref[0].astype(BF16)
    row = lax.broadcasted_iota(jnp.int32, (tq, tq), 0)
    col = lax.broadcasted_iota(jnp.int32, (tq, tq), 1)
    causal = col <= row

    for qi in range(s_len // tq):
        q = q_ref[0, qi * tq:(qi + 1) * tq, :]

        def step(kj, carry, masked):
            m, l, acc = carry
            start = pl.multiple_of(kj * tq, tq)
            kblk = kb_ref[pl.ds(start, tq), :]
            vblk = vb_ref[pl.ds(start, tq), :]
            s = lax.dot_general(q, kblk, (((1,), (1,)), ((), ())), preferred_element_type=F32)
            s = s * scale - ck_ref[0, kj]
            if masked:
                s = jnp.where(causal, s, -jnp.inf)
            m_new = jnp.maximum(m, jnp.max(s, axis=-1, keepdims=True))
            p = jnp.exp(s - m_new)
            alpha = jnp.exp(m - m_new)
            l = alpha * l + jnp.sum(p, axis=-1, keepdims=True)
            acc = alpha * acc + jnp.dot(p.astype(BF16), vblk, preferred_element_type=F32)
            return m_new, l, acc

        init = (jnp.full((tq, 1), -jnp.inf, F32), jnp.zeros((tq, 1), F32), jnp.zeros((tq, q.shape[1]), F32))
        carry = lax.fori_loop(0, qi, functools.partial(step, masked=False), init)
        _, l, acc = step(qi, carry, True)
        o_ref[0, qi * tq:(qi + 1) * tq, :] = (acc / l).astype(o_ref.dtype)


def _fox_attn(q, k, v, c, heads, *, name, tq_pref=512):
    b, s_len, d = q.shape
    dh = d // heads
    tq = _pick_tile(s_len, tq_pref, LANES)
    nk = s_len // tq
    ck = c.reshape(b * heads, nk, 1, tq)
    qkv_spec = pl.BlockSpec((1, s_len, dh), lambda bi, hi: (bi, 0, hi))
    est = 2 * s_len * dh * (2 + 4 + 4 + 2) + 2 * s_len * dh * 2 + 8 * tq * tq * 4
    return pl.pallas_call(
        functools.partial(_fox_attn_kernel, tq=tq, scale=dh ** -0.5),
        grid=(b, heads),
        in_specs=[qkv_spec, qkv_spec, qkv_spec,
                  pl.BlockSpec((1, nk, 1, tq), lambda bi, hi: (bi * heads + hi, 0, 0, 0))],
        out_specs=qkv_spec,
        out_shape=jax.ShapeDtypeStruct((b, s_len, d), BF16),
        scratch_shapes=[pltpu.VMEM((s_len, dh), BF16), pltpu.VMEM((s_len, dh), BF16)],
        compiler_params=_params(("parallel", "parallel"), est),
        name=name,
    )(q, k, v, ck)


def _fox_dec_kernel(q_ref, kn_ref, vn_ref, kc_ref, vc_ref, ckc_ref, ckn_ref, o_ref, *, scale):
    q = q_ref[0]
    t = q.shape[0]
    contract_last = (((1,), (1,)), ((), ()))
    s_c = lax.dot_general(q, kc_ref[0].astype(BF16), contract_last, preferred_element_type=F32) * scale - ckc_ref[0]
    s_n = lax.dot_general(q, kn_ref[0].astype(BF16), contract_last, preferred_element_type=F32) * scale - ckn_ref[0]
    causal = lax.broadcasted_iota(jnp.int32, (t, t), 1) <= lax.broadcasted_iota(jnp.int32, (t, t), 0)
    s_n = jnp.where(causal, s_n, -jnp.inf)
    m = jnp.maximum(jnp.max(s_c, axis=-1, keepdims=True), jnp.max(s_n, axis=-1, keepdims=True))
    p_c = jnp.exp(s_c - m)
    p_n = jnp.exp(s_n - m)
    l = jnp.sum(p_c, axis=-1, keepdims=True) + jnp.sum(p_n, axis=-1, keepdims=True)
    acc = (jnp.dot(p_c.astype(BF16), vc_ref[0].astype(BF16), preferred_element_type=F32)
           + jnp.dot(p_n.astype(BF16), vn_ref[0].astype(BF16), preferred_element_type=F32))
    o_ref[0] = (acc / l).astype(o_ref.dtype)


def _fox_dec_attn(q, k_new, v_new, k_cache, v_cache, c_cache, c_new, heads, *, name):
    b, t, d = q.shape
    p = k_cache.shape[1]
    dh = d // heads
    new_spec = pl.BlockSpec((1, t, dh), lambda bi, hi: (bi, 0, hi))
    cache_spec = pl.BlockSpec((1, p, dh), lambda bi, hi: (bi, 0, hi))
    est = 2 * 2 * p * dh * 4 + 2 * p * dh * 2 + 6 * t * p * 4
    return pl.pallas_call(
        functools.partial(_fox_dec_kernel, scale=dh ** -0.5),
        grid=(b, heads),
        in_specs=[new_spec, new_spec, new_spec, cache_spec, cache_spec,
                  pl.BlockSpec((1, 1, p), lambda bi, hi: (bi * heads + hi, 0, 0)),
                  pl.BlockSpec((1, 1, t), lambda bi, hi: (bi * heads + hi, 0, 0))],
        out_specs=new_spec,
        out_shape=jax.ShapeDtypeStruct((b, t, d), BF16),
        compiler_params=_params(("parallel", "parallel"), est),
        name=name,
    )(q, k_new, v_new, k_cache, v_cache, c_cache.reshape(b * heads, 1, p), c_new.reshape(b * heads, 1, t))


def _lru_kernel(u_ref, gate_ref, cw_ref, cb_ref, wa_ref, ba_ref, wx_ref, bx_ref, lam_ref, h0_ref, buf0_ref,
                y_ref, hl_ref, ul_ref, h_scr, tail_scr):
    @pl.when(pl.program_id(2) == 0)
    def _():
        h_scr[...] = h0_ref[0]
        tail_scr[...] = buf0_ref[0]

    u = u_ref[0]
    tc, bw = u.shape
    taps = cw_ref.shape[0]
    tail = tail_scr[...]
    head_rows = lax.broadcasted_iota(jnp.int32, (SUBLANES, bw), 0)

    def delayed(d):
        if d == 0:
            return u
        ru = pltpu.roll(u, d, axis=0)
        head = jnp.where(head_rows < d, pltpu.roll(tail, d, axis=0), ru[:SUBLANES])
        return head if tc == SUBLANES else jnp.concatenate([head, ru[SUBLANES:]], axis=0)

    conv = delayed(taps - 1) * cw_ref[0:1, :]
    for k in range(1, taps):
        conv = conv + delayed(taps - 1 - k) * cw_ref[k:k + 1, :]
    uc = cb_ref[...] + conv
    tail_scr[...] = u[tc - SUBLANES:, :]
    ul_ref[0] = u[tc - SUBLANES:, :]

    ucb = uc.astype(BF16)
    r = jax.nn.sigmoid(jnp.dot(ucb, wa_ref[0], preferred_element_type=F32) + ba_ref[...])
    ig = jax.nn.sigmoid(jnp.dot(ucb, wx_ref[0], preferred_element_type=F32) + bx_ref[...])
    log_a = (-LRU_C * _softplus(-lam_ref[...])) * r
    a = jnp.exp(log_a)
    th = jnp.tanh(log_a)
    b = jnp.sqrt((-2.0 * th) / (1.0 - th)) * (ig * uc)

    groups = tc // SUBLANES
    a3 = a.reshape(groups, SUBLANES, bw)
    b3 = b.reshape(groups, SUBLANES, bw)
    sub = lax.broadcasted_iota(jnp.int32, (groups, SUBLANES, bw), 1)
    d = 1
    while d < SUBLANES:
        valid = sub >= d
        b3 = jnp.where(valid, a3 * pltpu.roll(b3, d, axis=1) + b3, b3)
        a3 = jnp.where(valid, a3 * pltpu.roll(a3, d, axis=1), a3)
        d *= 2
    h = h_scr[...]
    hs = []
    for gi in range(groups):
        hg = a3[gi] * h + b3[gi]
        hs.append(hg)
        h = hg[SUBLANES - 1:SUBLANES, :]
    h_scr[...] = h
    hl_ref[0] = hs[-1]
    hs = hs[0] if groups == 1 else jnp.concatenate(hs, axis=0)
    y_ref[0] = (hs * gate_ref[0].astype(F32)).astype(y_ref.dtype)


def _lru_scan(u, gate, conv_w, conv_b, w_ga, b_a, w_gx, b_x, lam, h0, buf0, *, name, tc_pref=512):
    b, t, w = u.shape
    nb, bw, _ = w_ga.shape
    taps = conv_w.shape[0]
    tc = _pick_tile(t, tc_pref, SUBLANES)
    seq_spec = pl.BlockSpec((1, tc, bw), lambda bi, ni, ti: (bi, ti, ni))
    row_spec = pl.BlockSpec((1, bw), lambda bi, ni, ti: (0, ni))
    blk_spec = pl.BlockSpec((1, bw, bw), lambda bi, ni, ti: (ni, 0, 0))
    last_spec = pl.BlockSpec((1, SUBLANES, bw), lambda bi, ni, ti: (bi, 0, ni))
    est = 2 * tc * bw * (4 + 2 + 2) + 4 * bw * bw * 2 + 24 * tc * bw * 4
    return pl.pallas_call(
        _lru_kernel,
        grid=(b, nb, t // tc),
        in_specs=[seq_spec, seq_spec,
                  pl.BlockSpec((taps, bw), lambda bi, ni, ti: (0, ni)), row_spec,
                  blk_spec, row_spec, blk_spec, row_spec, row_spec,
                  pl.BlockSpec((1, 1, bw), lambda bi, ni, ti: (bi, 0, ni)), last_spec],
        out_specs=[seq_spec, last_spec, last_spec],
        out_shape=[jax.ShapeDtypeStruct((b, t, w), BF16),
                   jax.ShapeDtypeStruct((b, SUBLANES, w), F32),
                   jax.ShapeDtypeStruct((b, SUBLANES, w), F32)],
        scratch_shapes=[pltpu.VMEM((1, bw), F32), pltpu.VMEM((SUBLANES, bw), F32)],
        compiler_params=_params(("parallel", "parallel", "arbitrary"), est),
        name=name,
    )(u, gate, conv_w, conv_b.reshape(1, w), w_ga, b_a.reshape(1, w), w_gx, b_x.reshape(1, w),
      lam.reshape(1, w), h0, buf0)


def _xattn_kernel(x_ref, g_ref, wq_ref, mk_ref, mv_ref, wo_ref, o_ref, *, heads):
    x = x_ref[0]
    h = _rms(x, g_ref[...]).astype(BF16)
    q = jnp.dot(h, wq_ref[...], preferred_element_type=F32).astype(BF16)
    dh = q.shape[1] // heads
    scale = dh ** -0.5
    outs = []
    for hd in range(heads):
        cols = slice(hd * dh, (hd + 1) * dh)
        s = lax.dot_general(q[:, cols], mk_ref[0, :, cols].astype(BF16), (((1,), (1,)), ((), ())),
                            preferred_element_type=F32) * scale
        p = jnp.exp(s - jnp.max(s, axis=-1, keepdims=True))
        l = jnp.sum(p, axis=-1, keepdims=True)
        o = jnp.dot(p.astype(BF16), mv_ref[0, :, cols].astype(BF16), preferred_element_type=F32)
        outs.append((o / l).astype(BF16))
    o_all = jnp.concatenate(outs, axis=-1)
    o_ref[0] = x + jnp.dot(o_all, wo_ref[...], preferred_element_type=F32)


def _xattn(x, g, wq, mk, mv, wo, heads, *, name, tm_pref=512):
    b, t, d = x.shape
    nm, mw = mk.shape[1], mk.shape[2]
    tm = _pick_tile(t, tm_pref, SUBLANES)
    x_spec = pl.BlockSpec((1, tm, d), lambda bi, ti: (bi, ti, 0))
    mem_spec = pl.BlockSpec((1, nm, mw), lambda bi, ti: (bi, 0, 0))
    est = 4 * tm * d * 4 + 4 * d * mw * 2 + 4 * nm * mw * 4 + 3 * tm * d * 4
    return pl.pallas_call(
        functools.partial(_xattn_kernel, heads=heads),
        grid=(b, t // tm),
        in_specs=[x_spec, pl.BlockSpec((1, d), lambda bi, ti: (0, 0)),
                  pl.BlockSpec((d, mw), lambda bi, ti: (0, 0)), mem_spec, mem_spec,
                  pl.BlockSpec((mw, d), lambda bi, ti: (0, 0))],
        out_specs=x_spec,
        out_shape=jax.ShapeDtypeStruct((b, t, d), F32),
        compiler_params=_params(("parallel", "parallel"), est),
        name=name,
    )(x, g.reshape(1, d), wq, mk, mv, wo)


def _ffn_kernel(x_ref, g_ref, wg_ref, wu_ref, wo_ref, gf_ref, o_ref, h_ref, *, final_norm):
    j = pl.program_id(1)

    @pl.when(j == 0)
    def _():
        x = x_ref[...]
        h_ref[...] = _rms(x, g_ref[...]).astype(BF16)
        o_ref[...] = x

    h = h_ref[...]
    gate = jnp.dot(h, wg_ref[...], preferred_element_type=F32)
    up = jnp.dot(h, wu_ref[...], preferred_element_type=F32)
    act = (jax.nn.silu(gate) * up).astype(BF16)
    o_ref[...] += jnp.dot(act, wo_ref[...], preferred_element_type=F32)

    if final_norm:
        @pl.when(j == pl.num_programs(1) - 1)
        def _():
            o_ref[...] = _rms(o_ref[...], gf_ref[...])


def _ffn(x, g, w_in, w_out, g_final, *, final_norm, name, tm_pref=512, tf_pref=512):
    m, d = x.shape
    f = w_out.shape[0]
    tm = _pick_tile(m, tm_pref, SUBLANES)
    tf = _pick_tile(f, tf_pref, LANES)
    nf = f // tf
    x_spec = pl.BlockSpec((tm, d), lambda i, j: (i, 0))
    vec_spec = pl.BlockSpec((1, d), lambda i, j: (0, 0))
    est = 4 * tm * d * 4 + tm * d * 2 + 2 * 3 * d * tf * 2 + 4 * tm * tf * 4
    return pl.pallas_call(
        functools.partial(_ffn_kernel, final_norm=final_norm),
        grid=(m // tm, nf),
        in_specs=[x_spec, vec_spec,
                  pl.BlockSpec((d, tf), lambda i, j: (0, j)),
                  pl.BlockSpec((d, tf), lambda i, j: (0, nf + j)),
                  pl.BlockSpec((tf, d), lambda i, j: (j, 0)), vec_spec],
        out_specs=x_spec,
        out_shape=jax.ShapeDtypeStruct((m, d), F32),
        scratch_shapes=[pltpu.VMEM((tm, d), BF16)],
        compiler_params=_params(("parallel", "arbitrary"), est),
        name=name,
    )(x, g.reshape(1, d), w_in, w_in, w_out, g_final.reshape(1, d))


def kernel(x_prompt, x_sample, mem_prompt, cache_fox_k, cache_fox_v, cache_fox_logf, cache_mem_k, cache_mem_v, state_lru_h, state_lru_conv, norm_mix, norm_mem, norm_xattn, norm_ffn, norm_final, fox_w_in, fox_b_f, fox_w_out, lru_w_in, lru_conv_w, lru_conv_b, lru_w_ga, lru_b_a, lru_w_gx, lru_b_x, lru_lambda, lru_w_out, xattn_w_q, xattn_w_kv, xattn_w_o, ffn_w_in, ffn_w_out):
    b, s_len, d = x_prompt.shape
    bd, t_dec, _ = x_sample.shape
    depth = norm_mix.shape[0]
    heads = fox_b_f.shape[1]
    past = cache_fox_k.shape[2]
    n_mem, mem_heads = cache_mem_k.shape[2], cache_mem_k.shape[3]
    mem_w = mem_heads * cache_mem_k.shape[4]
    taps = lru_conv_w.shape[1]
    assert taps - 1 <= SUBLANES <= min(s_len, t_dec) and heads <= LANES

    xp = x_prompt.reshape(b * s_len, d)
    xs = x_sample.reshape(bd * t_dec, d)
    mem = mem_prompt.reshape(b * n_mem, d)
    pk, pv, plf, pmk, pmv, ph, pc = [], [], [], [], [], [], []
    sk, sv, slf, sh, sc = [], [], [], [], []

    for i in range(depth):
        j = i // 2
        if i % 2 == 0:
            w_in = fox_w_in[j]
            w_qkv = w_in[:, :3 * d].astype(BF16)
            w_f = jnp.pad(w_in[:, 3 * d:], ((0, 0), (0, LANES - heads))).astype(BF16)
            b_f = jnp.pad(fox_b_f[j], (0, LANES - heads)).reshape(1, LANES)
            w_out = fox_w_out[j].astype(BF16)
            proj = functools.partial(_norm_proj, g=norm_mix[i], w=w_qkv, acts=[_identity] * 3,
                                     dtypes=[BF16, F32, F32], extra=(w_f, b_f, heads, _log_sigmoid))

            qp, kp, vp, lfp = proj(xp, name=f"fox_proj_p{i}")
            lf_t = lfp.reshape(b, s_len, heads).transpose(0, 2, 1).reshape(b * heads, s_len)
            c_p = _cumsum_lanes(lf_t, name=f"fox_cumsum_p{i}")
            attn = _fox_attn(qp.reshape(b, s_len, d), kp.reshape(b, s_len, d), vp.reshape(b, s_len, d), c_p, heads,
                             name=f"fox_attn_p{i}")
            xp = _matmul_res(attn.reshape(b * s_len, d), w_out, xp, name=f"fox_out_p{i}")
            pk.append(kp.reshape(b, s_len, heads, d // heads))
            pv.append(vp.reshape(b, s_len, heads, d // heads))
            plf.append(lfp.reshape(b, s_len, heads))

            qs, ks, vs, lfs = proj(xs, name=f"fox_proj_s{i}")
            lf_all = jnp.concatenate([cache_fox_logf[j], lfs.reshape(bd, t_dec, heads)], axis=1)
            total = past + t_dec
            padded = -(-total // LANES) * LANES
            lf_t = jnp.pad(lf_all.transpose(0, 2, 1).reshape(bd * heads, total), ((0, 0), (0, padded - total)))
            c_s = _cumsum_lanes(lf_t, name=f"fox_cumsum_s{i}")
            attn = _fox_dec_attn(qs.reshape(bd, t_dec, d), ks.reshape(bd, t_dec, d), vs.reshape(bd, t_dec, d),
                                 cache_fox_k[j].reshape(bd, past, d), cache_fox_v[j].reshape(bd, past, d),
                                 c_s[:, :past], c_s[:, past:total], heads, name=f"fox_attn_s{i}")
            xs = _matmul_res(attn.reshape(bd * t_dec, d), w_out, xs, name=f"fox_out_s{i}")
            sk.append(ks.reshape(bd, t_dec, heads, d // heads))
            sv.append(vs.reshape(bd, t_dec, heads, d // heads))
            slf.append(lfs.reshape(bd, t_dec, heads))
        else:
            w_in = lru_w_in[j].astype(BF16)
            width = w_in.shape[1] // 2
            w_ga, w_gx = lru_w_ga[j].astype(BF16), lru_w_gx[j].astype(BF16)
            w_out = lru_w_out[j].astype(BF16)
            proj = functools.partial(_norm_proj, g=norm_mix[i], w=w_in, acts=[jax.nn.gelu, _identity], dtypes=[BF16, F32])
            scan = functools.partial(_lru_scan, conv_w=lru_conv_w[j], conv_b=lru_conv_b[j], w_ga=w_ga, b_a=lru_b_a[j],
                                     w_gx=w_gx, b_x=lru_b_x[j], lam=lru_lambda[j])

            gate, u = proj(xp, name=f"lru_proj_p{i}")
            y, h8, u8 = scan(u.reshape(b, s_len, width), gate.reshape(b, s_len, width),
                             h0=jnp.zeros((b, 1, width), F32), buf0=jnp.zeros((b, SUBLANES, width), F32),
                             name=f"lru_scan_p{i}")
            xp = _matmul_res(y.reshape(b * s_len, width), w_out, xp, name=f"lru_out_p{i}")
            ph.append(h8[:, SUBLANES - 1])
            pc.append(u8[:, SUBLANES - (taps - 1):])

            gate, u = proj(xs, name=f"lru_proj_s{i}")
            buf0 = jnp.pad(state_lru_conv[j], ((0, 0), (SUBLANES - (taps - 1), 0), (0, 0)))
            y, h8, u8 = scan(u.reshape(bd, t_dec, width), gate.reshape(bd, t_dec, width),
                             h0=state_lru_h[j].reshape(bd, 1, width), buf0=buf0, name=f"lru_scan_s{i}")
            xs = _matmul_res(y.reshape(bd * t_dec, width), w_out, xs, name=f"lru_out_s{i}")
            sh.append(h8[:, SUBLANES - 1])
            sc.append(u8[:, SUBLANES - (taps - 1):])

        mk, mv = _norm_proj(mem, norm_mem[i], xattn_w_kv[i].astype(BF16), [_identity] * 2, [F32, F32], name=f"mem_kv{i}")
        pmk.append(mk.reshape(b, n_mem, mem_heads, mem_w // mem_heads))
        pmv.append(mv.reshape(b, n_mem, mem_heads, mem_w // mem_heads))
        wq, wo = xattn_w_q[i].astype(BF16), xattn_w_o[i].astype(BF16)
        xp = _xattn(xp.reshape(b, s_len, d), norm_xattn[i], wq, mk.reshape(b, n_mem, mem_w), mv.reshape(b, n_mem, mem_w),
                    wo, mem_heads, name=f"xattn_p{i}").reshape(b * s_len, d)
        xs = _xattn(xs.reshape(bd, t_dec, d), norm_xattn[i], wq, cache_mem_k[i].reshape(bd, n_mem, mem_w),
                    cache_mem_v[i].reshape(bd, n_mem, mem_w), wo, mem_heads, name=f"xattn_s{i}").reshape(bd * t_dec, d)

        last = i == depth - 1
        w_in, w_out = ffn_w_in[i].astype(BF16), ffn_w_out[i].astype(BF16)
        xp = _ffn(xp, norm_ffn[i], w_in, w_out, norm_final, final_norm=last, name=f"ffn_p{i}")
        xs = _ffn(xs, norm_ffn[i], w_in, w_out, norm_final, final_norm=last, name=f"ffn_s{i}")

    return (xp.reshape(b, s_len, d), xs.reshape(bd, t_dec, d),
            jnp.stack(pk), jnp.stack(pv), jnp.stack(plf), jnp.stack(pmk), jnp.stack(pmv),
            jnp.stack(ph), jnp.stack(pc),
            jnp.stack(sk), jnp.stack(sv), jnp.stack(slf), jnp.stack(sh), jnp.stack(sc))
```

```python
import functools

import jax
import jax.numpy as jnp
from jax import lax
from jax.experimental import pallas as pl
from jax.experimental.pallas import tpu as pltpu

F32 = jnp.float32
BF16 = jnp.bfloat16

EPS = 1e-6
LOG2E = 1.4426950408889634
LRU_C = 8.0
LANES = 128
SUBLANES = 8
V7X_VMEM_BUDGET = 56 * 1024 * 1024


def _pick_tile(n, pref, align):
    if n <= pref:
        return n
    t = (pref // align) * align
    while t >= align:
        if n % t == 0:
            return t
        t -= align
    raise ValueError(f"no {align}-aligned tile of {n} below {pref}")


def _params(semantics, vmem_estimate):
    limit = int(min(V7X_VMEM_BUDGET, max(2 * vmem_estimate, 16 * 1024 * 1024)))
    return pltpu.CompilerParams(dimension_semantics=semantics, vmem_limit_bytes=limit)


def _rms(xf, g):
    return xf * lax.rsqrt(jnp.mean(xf * xf, axis=-1, keepdims=True) + EPS) * g


def _identity(x):
    return x


def _softplus(x):
    return jnp.maximum(x, 0.0) + jnp.log1p(jnp.exp(-jnp.abs(x)))


def _log_sigmoid(x):
    return -_softplus(-x)


def _norm_proj_kernel(*refs, acts, extra_act):
    n_sec = len(acts)
    x_ref, g_ref = refs[0], refs[1]
    w_refs = refs[2:2 + n_sec]
    pos = 2 + n_sec
    if extra_act is not None:
        we_ref, be_ref = refs[pos], refs[pos + 1]
        pos += 2
    o_refs = refs[pos:pos + n_sec]
    pos += n_sec
    if extra_act is not None:
        e_ref = refs[pos]
        pos += 1
    h_ref = refs[pos]

    @pl.when(pl.program_id(1) == 0)
    def _():
        h = _rms(x_ref[...], g_ref[...]).astype(BF16)
        h_ref[...] = h
        if extra_act is not None:
            z = jnp.dot(h, we_ref[...], preferred_element_type=F32) + be_ref[...]
            e_ref[...] = extra_act(z)[:, :e_ref.shape[1]]

    h = h_ref[...]
    for w_ref, o_ref, act in zip(w_refs, o_refs, acts):
        o_ref[...] = act(jnp.dot(h, w_ref[...], preferred_element_type=F32)).astype(o_ref.dtype)


def _norm_proj(x, g, w, acts, dtypes, *, name, extra=None, tm_pref=1024, tn_pref=256):
    m, d = x.shape
    n_sec = len(acts)
    ns = w.shape[1] // n_sec
    tm = _pick_tile(m, tm_pref, SUBLANES)
    tn = _pick_tile(ns, tn_pref, LANES)
    tiles = ns // tn
    in_specs = [pl.BlockSpec((tm, d), lambda i, j: (i, 0)),
                pl.BlockSpec((1, d), lambda i, j: (0, 0))]
    in_specs += [pl.BlockSpec((d, tn), lambda i, j, s=s: (0, s * tiles + j)) for s in range(n_sec)]
    args = [x, g.reshape(1, d)] + [w] * n_sec
    out_specs = [pl.BlockSpec((tm, tn), lambda i, j: (i, j)) for _ in range(n_sec)]
    out_shape = [jax.ShapeDtypeStruct((m, ns), dt) for dt in dtypes]
    extra_act = None
    if extra is not None:
        w_e, b_e, n_e, extra_act = extra
        in_specs += [pl.BlockSpec(w_e.shape, lambda i, j: (0, 0)), pl.BlockSpec(b_e.shape, lambda i, j: (0, 0))]
        args += [w_e, b_e]
        out_specs.append(pl.BlockSpec((tm, n_e), lambda i, j: (i, 0)))
        out_shape.append(jax.ShapeDtypeStruct((m, n_e), F32))
    est = 2 * tm * d * 4 + tm * d * 2 + n_sec * 2 * (d * tn * 2 + tm * tn * 4) + 4 * tm * tn * 4
    return pl.pallas_call(
        functools.partial(_norm_proj_kernel, acts=tuple(acts), extra_act=extra_act),
        grid=(m // tm, tiles),
        in_specs=in_specs, out_specs=out_specs, out_shape=out_shape,
        scratch_shapes=[pltpu.VMEM((tm, d), BF16)],
        compiler_params=_params(("parallel", "arbitrary"), est),
        name=name,
    )(*args)


def _matmul_res_kernel(a_ref, w_ref, r_ref, o_ref, *, tn):
    a = a_ref[...]
    for n0 in range(0, o_ref.shape[1], tn):
        o_ref[:, n0:n0 + tn] = r_ref[:, n0:n0 + tn] + jnp.dot(a, w_ref[:, n0:n0 + tn], preferred_element_type=F32)


def _matmul_res(a, w, res, *, name, tm_pref=512, tn_pref=512):
    m, k = a.shape
    n = w.shape[1]
    tm = _pick_tile(m, tm_pref, SUBLANES)
    tn = _pick_tile(n, tn_pref, LANES)
    est = 2 * (tm * k * 2 + k * n * 2 + 2 * tm * n * 4) + 2 * tm * tn * 4
    return pl.pallas_call(
        functools.partial(_matmul_res_kernel, tn=tn),
        grid=(m // tm,),
        in_specs=[pl.BlockSpec((tm, k), lambda i: (i, 0)),
                  pl.BlockSpec((k, n), lambda i: (0, 0)),
                  pl.BlockSpec((tm, n), lambda i: (i, 0))],
        out_specs=pl.BlockSpec((tm, n), lambda i: (i, 0)),
        out_shape=jax.ShapeDtypeStruct((m, n), F32),
        compiler_params=_params(("parallel",), est),
        name=name,
    )(a, w, res)


def _cumsum_kernel(x_ref, o_ref, carry_ref):
    @pl.when(pl.program_id(0) == 0)
    def _():
        carry_ref[...] = jnp.zeros_like(carry_ref)

    x = x_ref[...]
    n = x.shape[1]
    upper = (lax.broadcasted_iota(jnp.int32, (n, n), 0) <= lax.broadcasted_iota(jnp.int32, (n, n), 1)).astype(BF16)
    hi = x.astype(BF16)
    rest = x - hi.astype(F32)
    mid = rest.astype(BF16)
    lo = (rest - mid.astype(F32)).astype(BF16)
    local = (jnp.dot(hi, upper, preferred_element_type=F32) + jnp.dot(mid, upper, preferred_element_type=F32)
             + jnp.dot(lo, upper, preferred_element_type=F32))
    out = local + carry_ref[...]
    o_ref[...] = out
    carry_ref[...] = jnp.broadcast_to(out[:, n - 1:n], out.shape)


def _cumsum_lanes(x, *, name):
    r, l = x.shape
    return pl.pallas_call(
        _cumsum_kernel,
        grid=(l // LANES,),
        in_specs=[pl.BlockSpec((r, LANES), lambda t: (0, t))],
        out_specs=pl.BlockSpec((r, LANES), lambda t: (0, t)),
        out_shape=jax.ShapeDtypeStruct((r, l), F32),
        scratch_shapes=[pltpu.VMEM((r, LANES), F32)],
        compiler_params=_params(("arbitrary",), 8 * r * LANES * 4),
        name=name,
    )(x)


def _cumsum_rows_kernel(x_ref, o_ref, carry_ref):
    @pl.when(pl.program_id(1) == 0)
    def _():
        carry_ref[...] = jnp.zeros_like(carry_ref)

    x = x_ref[0]
    n = x.shape[0]
    lower = (lax.broadcasted_iota(jnp.int32, (n, n), 0) >= lax.broadcasted_iota(jnp.int32, (n, n), 1)).astype(BF16)
    hi = x.astype(BF16)
    rest = x - hi.astype(F32)
    mid = rest.astype(BF16)
    lo = (rest - mid.astype(F32)).astype(BF16)
    local = (jnp.dot(lower, hi, preferred_element_type=F32) + jnp.dot(lower, mid, preferred_element_type=F32)
             + jnp.dot(lower, lo, preferred_element_type=F32))
    out = local + carry_ref[...]
    o_ref[0] = out
    carry_ref[...] = out[n - 1:n, :]


def _cumsum_rows(x, *, name, tile_pref=256):
    b, s_len, w = x.shape
    tile = _pick_tile(s_len, tile_pref, SUBLANES)
    spec = pl.BlockSpec((1, tile, w), lambda bi, ti: (bi, ti, 0))
    return pl.pallas_call(
        _cumsum_rows_kernel,
        grid=(b, s_len // tile),
        in_specs=[spec], out_specs=spec,
        out_shape=jax.ShapeDtypeStruct((b, s_len, w), F32),
        scratch_shapes=[pltpu.VMEM((1, w), F32)],
        compiler_params=_params(("parallel", "arbitrary"), 8 * tile * w * 4 + 4 * tile * tile),
        name=name,
    )(x)


def _fox_attn_kernel(q_ref, k_ref, v_ref, c_ref, o_ref, kaug_ref, vt_ref, ta_ref, tb_ref, pa_ref, pb_ref,
                     acc_ref, l8_ref, m_ref, *, tq, heads, scale, unroll):
    s_len, dh = q_ref.shape[1], q_ref.shape[2]
    nk = s_len // tq
    h = pl.program_id(1)
    c2 = scale * LOG2E

    @pl.when(h == 0)
    def _():
        head_lane = lax.broadcasted_iota(jnp.int32, c_ref.shape[1:], 1) < heads
        ckp = jnp.where(head_lane, c_ref[0] * (-1.0 / scale), 0.0)
        hi = ckp.astype(BF16).astype(F32)
        rest = ckp - hi
        mid = rest.astype(BF16).astype(F32)
        lo = (rest - mid).astype(BF16).astype(F32)
        aug = hi + pltpu.roll(mid, heads, axis=1) + pltpu.roll(lo, 2 * heads, axis=1)
        kaug_ref[:, dh:] = aug.astype(BF16)

    kaug_ref[:, :dh] = k_ref[0].astype(BF16)
    for kj in range(nk):
        vt_ref[kj] = v_ref[0, kj * tq:(kj + 1) * tq, :].T.astype(BF16)

    lane = lax.broadcasted_iota(jnp.int32, (tq, LANES), 1)
    pick = ((lane == h) | (lane == heads + h) | (lane == 2 * heads + h)).astype(BF16)
    contract_last = (((1,), (1,)), ((), ()))
    t_refs, p_refs = (ta_ref, tb_ref), (pa_ref, pb_ref)

    m_ref[...] = jnp.full(m_ref.shape, -jnp.inf, F32)
    l8_ref[...] = jnp.zeros(l8_ref.shape, F32)
    acc_ref[...] = jnp.zeros(acc_ref.shape, F32)

    def scores(pair, masked, t_ref):
        qa, ka = pair
        q = q_ref[0, pl.ds(pl.multiple_of(qa * tq, tq), tq), :]
        kblk = kaug_ref[pl.ds(pl.multiple_of(ka * tq, tq), tq), :]
        t = lax.dot_general(kblk, jnp.concatenate([q, pick], axis=1), contract_last, preferred_element_type=F32)
        if masked:
            causal = lax.broadcasted_iota(jnp.int32, (tq, tq), 0) <= lax.broadcasted_iota(jnp.int32, (tq, tq), 1)
            t = jnp.where(causal, t, -jnp.inf)
        t_ref[...] = t
        return jnp.max(t, axis=0, keepdims=True)

    def softmax_update(pair, t_ref, p_ref, m_blk):
        qa, _ = pair
        m_prev = m_ref[qa]
        m_new = jnp.maximum(m_prev, m_blk)
        alpha = jnp.exp2((m_prev - m_new) * c2)
        p = jnp.exp2((t_ref[...] - m_new) * c2)
        p_ref[...] = p.astype(BF16)
        l8_ref[qa] = alpha * l8_ref[qa] + jnp.sum(p.reshape(tq // SUBLANES, SUBLANES, tq), axis=0)
        m_ref[qa] = m_new
        return alpha

    def accumulate(pair, p_ref, alpha):
        qa, ka = pair
        acc_ref[qa] = alpha * acc_ref[qa] + jnp.dot(vt_ref[ka], p_ref[...], preferred_element_type=F32)

    def pipeline_step(e, pairs, masked, state):
        pair_a, pair_b, pair_c = pairs
        m_blk, alpha = state
        if pair_c is not None:
            accumulate(pair_c, p_refs[e % 2], alpha)
        if pair_b is not None:
            alpha = softmax_update(pair_b, t_refs[(e - 1) % 2], p_refs[(e - 1) % 2], m_blk)
        if pair_a is not None:
            m_blk = scores(pair_a, masked, t_refs[e % 2])
        return m_blk, alpha

    static_pairs = [(qa, qa) for qa in range(nk)] + [(qa, ka) for qa in range(nk) for ka in range(qa)]
    n_steps = len(static_pairs)
    loop_start = nk + nk % 2
    n_loop = max(n_steps - loop_start, 0) // unroll

    def static_step(e, state):
        pairs = [static_pairs[i] if 0 <= i < n_steps else None for i in (e, e - 1, e - 2)]
        return pipeline_step(e, pairs, e < nk, state)

    def advance(pair):
        qa, ka = pair
        wrap = ka + 1 == qa
        return jnp.where(wrap, qa + 1, qa), jnp.where(wrap, 0, ka + 1)

    def body(_, carry):
        state, pairs = carry
        for e in range(unroll):
            state = pipeline_step(e, pairs, False, state)
            pairs = (advance(pairs[0]), pairs[0], pairs[1])
        return state, pairs

    state = (jnp.zeros((1, tq), F32), jnp.zeros((1, tq), F32))
    for e in range(loop_start):
        state = static_step(e, state)
    if n_loop:
        first = tuple((jnp.int32(qa), jnp.int32(ka)) for qa, ka in (static_pairs[loop_start - i] for i in range(3)))
        state, _ = lax.fori_loop(0, n_loop, body, (state, first))
    for e in range(loop_start + unroll * n_loop, n_steps + 2):
        state = static_step(e, state)

    for qi in range(nk):
        out_t = acc_ref[qi] / jnp.sum(l8_ref[qi], axis=0, keepdims=True)
        o_ref[0, qi * tq:(qi + 1) * tq, :] = out_t.T.astype(o_ref.dtype)


def _fox_attn(q, k, v, c, heads, *, name, tq_pref=512, unroll=2):
    b, s_len, d = q.shape
    dh = d // heads
    assert unroll % 2 == 0 and 3 * heads <= LANES
    tq = _pick_tile(s_len, tq_pref, LANES)
    nk = s_len // tq
    qkv_spec = pl.BlockSpec((1, s_len, dh), lambda bi, hi: (bi, 0, hi))
    est = (2 * s_len * dh * (2 + 4 + 4 + 2) + 2 * s_len * LANES * 4 + s_len * (2 * dh + LANES) * 2
           + s_len * (dh + SUBLANES + 1) * 4 + 3 * tq * tq * 4 + 8 * tq * tq * 4)
    return pl.pallas_call(
        functools.partial(_fox_attn_kernel, tq=tq, heads=heads, scale=dh ** -0.5, unroll=unroll),
        grid=(b, heads),
        in_specs=[qkv_spec, qkv_spec, qkv_spec,
                  pl.BlockSpec((1, s_len, LANES), lambda bi, hi: (bi, 0, 0))],
        out_specs=qkv_spec,
        out_shape=jax.ShapeDtypeStruct((b, s_len, d), BF16),
        scratch_shapes=[pltpu.VMEM((s_len, dh + LANES), BF16), pltpu.VMEM((nk, dh, tq), BF16),
                        pltpu.VMEM((tq, tq), F32), pltpu.VMEM((tq, tq), F32),
                        pltpu.VMEM((tq, tq), BF16), pltpu.VMEM((tq, tq), BF16),
                        pltpu.VMEM((nk, dh, tq), F32), pltpu.VMEM((nk, SUBLANES, tq), F32),
                        pltpu.VMEM((nk, 1, tq), F32)],
        compiler_params=_params(("parallel", "arbitrary"), est),
        name=name,
    )(q, k, v, c)


def _fox_dec_kernel(q_ref, kn_ref, vn_ref, kc_ref, vc_ref, ckc_ref, ckn_ref, o_ref, *, scale):
    q = q_ref[0]
    t = q.shape[0]
    contract_last = (((1,), (1,)), ((), ()))
    s_c = lax.dot_general(q, kc_ref[0].astype(BF16), contract_last, preferred_element_type=F32) * scale - ckc_ref[0]
    s_n = lax.dot_general(q, kn_ref[0].astype(BF16), contract_last, preferred_element_type=F32) * scale - ckn_ref[0]
    causal = lax.broadcasted_iota(jnp.int32, (t, t), 1) <= lax.broadcasted_iota(jnp.int32, (t, t), 0)
    s_n = jnp.where(causal, s_n, -jnp.inf)
    m = jnp.maximum(jnp.max(s_c, axis=-1, keepdims=True), jnp.max(s_n, axis=-1, keepdims=True))
    p_c = jnp.exp(s_c - m)
    p_n = jnp.exp(s_n - m)
    l = jnp.sum(p_c, axis=-1, keepdims=True) + jnp.sum(p_n, axis=-1, keepdims=True)
    acc = (jnp.dot(p_c.astype(BF16), vc_ref[0].astype(BF16), preferred_element_type=F32)
           + jnp.dot(p_n.astype(BF16), vn_ref[0].astype(BF16), preferred_element_type=F32))
    o_ref[0] = (acc / l).astype(o_ref.dtype)


def _fox_dec_attn(q, k_new, v_new, k_cache, v_cache, c_cache, c_new, heads, *, name):
    b, t, d = q.shape
    p = k_cache.shape[1]
    dh = d // heads
    new_spec = pl.BlockSpec((1, t, dh), lambda bi, hi: (bi, 0, hi))
    cache_spec = pl.BlockSpec((1, p, dh), lambda bi, hi: (bi, 0, hi))
    est = 2 * 2 * p * dh * 4 + 2 * p * dh * 2 + 6 * t * p * 4
    return pl.pallas_call(
        functools.partial(_fox_dec_kernel, scale=dh ** -0.5),
        grid=(b, heads),
        in_specs=[new_spec, new_spec, new_spec, cache_spec, cache_spec,
                  pl.BlockSpec((1, 1, p), lambda bi, hi: (bi * heads + hi, 0, 0)),
                  pl.BlockSpec((1, 1, t), lambda bi, hi: (bi * heads + hi, 0, 0))],
        out_specs=new_spec,
        out_shape=jax.ShapeDtypeStruct((b, t, d), BF16),
        compiler_params=_params(("parallel", "parallel"), est),
        name=name,
    )(q, k_new, v_new, k_cache, v_cache, c_cache.reshape(b * heads, 1, p), c_new.reshape(b * heads, 1, t))


def _lru_kernel(u_ref, gate_ref, cw_ref, cb_ref, wa_ref, ba_ref, wx_ref, bx_ref, lam_ref, h0_ref, buf0_ref,
                y_ref, hl_ref, ul_ref, h_scr, tail_scr):
    @pl.when(pl.program_id(2) == 0)
    def _():
        h_scr[...] = h0_ref[0]
        tail_scr[...] = buf0_ref[0]

    u = u_ref[0]
    tc, bw = u.shape
    taps = cw_ref.shape[0]
    tail = tail_scr[...]
    head_rows = lax.broadcasted_iota(jnp.int32, (SUBLANES, bw), 0)

    def delayed(d):
        if d == 0:
            return u
        ru = pltpu.roll(u, d, axis=0)
        head = jnp.where(head_rows < d, pltpu.roll(tail, d, axis=0), ru[:SUBLANES])
        return head if tc == SUBLANES else jnp.concatenate([head, ru[SUBLANES:]], axis=0)

    conv = delayed(taps - 1) * cw_ref[0:1, :]
    for k in range(1, taps):
        conv = conv + delayed(taps - 1 - k) * cw_ref[k:k + 1, :]
    uc = cb_ref[...] + conv
    tail_scr[...] = u[tc - SUBLANES:, :]
    ul_ref[0] = u[tc - SUBLANES:, :]

    ucb = uc.astype(BF16)
    r = jax.nn.sigmoid(jnp.dot(ucb, wa_ref[0], preferred_element_type=F32) + ba_ref[...])
    ig = jax.nn.sigmoid(jnp.dot(ucb, wx_ref[0], preferred_element_type=F32) + bx_ref[...])
    log_a = (-LRU_C * _softplus(-lam_ref[...])) * r
    a = jnp.exp(log_a)
    th = jnp.tanh(log_a)
    b = jnp.sqrt((-2.0 * th) / (1.0 - th)) * (ig * uc)

    groups = tc // SUBLANES
    a3 = a.reshape(groups, SUBLANES, bw)
    b3 = b.reshape(groups, SUBLANES, bw)
    sub = lax.broadcasted_iota(jnp.int32, (groups, SUBLANES, bw), 1)
    d = 1
    while d < SUBLANES:
        valid = sub >= d
        b3 = jnp.where(valid, a3 * pltpu.roll(b3, d, axis=1) + b3, b3)
        a3 = jnp.where(valid, a3 * pltpu.roll(a3, d, axis=1), a3)
        d *= 2
    h = h_scr[...]
    hs = []
    for gi in range(groups):
        hg = a3[gi] * h + b3[gi]
        hs.append(hg)
        h = hg[SUBLANES - 1:SUBLANES, :]
    h_scr[...] = h
    hl_ref[0] = hs[-1]
    hs = hs[0] if groups == 1 else jnp.concatenate(hs, axis=0)
    y_ref[0] = (hs * gate_ref[0].astype(F32)).astype(y_ref.dtype)


def _lru_scan(u, gate, conv_w, conv_b, w_ga, b_a, w_gx, b_x, lam, h0, buf0, *, name, tc_pref=512):
    b, t, w = u.shape
    nb, bw, _ = w_ga.shape
    taps = conv_w.shape[0]
    tc = _pick_tile(t, tc_pref, SUBLANES)
    seq_spec = pl.BlockSpec((1, tc, bw), lambda bi, ni, ti: (bi, ti, ni))
    row_spec = pl.BlockSpec((1, bw), lambda bi, ni, ti: (0, ni))
    blk_spec = pl.BlockSpec((1, bw, bw), lambda bi, ni, ti: (ni, 0, 0))
    last_spec = pl.BlockSpec((1, SUBLANES, bw), lambda bi, ni, ti: (bi, 0, ni))
    est = 2 * tc * bw * (4 + 2 + 2) + 4 * bw * bw * 2 + 24 * tc * bw * 4
    return pl.pallas_call(
        _lru_kernel,
        grid=(b, nb, t // tc),
        in_specs=[seq_spec, seq_spec,
                  pl.BlockSpec((taps, bw), lambda bi, ni, ti: (0, ni)), row_spec,
                  blk_spec, row_spec, blk_spec, row_spec, row_spec,
                  pl.BlockSpec((1, 1, bw), lambda bi, ni, ti: (bi, 0, ni)), last_spec],
        out_specs=[seq_spec, last_spec, last_spec],
        out_shape=[jax.ShapeDtypeStruct((b, t, w), BF16),
                   jax.ShapeDtypeStruct((b, SUBLANES, w), F32),
                   jax.ShapeDtypeStruct((b, SUBLANES, w), F32)],
        scratch_shapes=[pltpu.VMEM((1, bw), F32), pltpu.VMEM((SUBLANES, bw), F32)],
        compiler_params=_params(("parallel", "parallel", "arbitrary"), est),
        name=name,
    )(u, gate, conv_w, conv_b.reshape(1, w), w_ga, b_a.reshape(1, w), w_gx, b_x.reshape(1, w),
      lam.reshape(1, w), h0, buf0)


def _xattn_kernel(x_ref, g_ref, wq_ref, mk_ref, mv_ref, wo_ref, o_ref, *, heads):
    x = x_ref[0]
    h = _rms(x, g_ref[...]).astype(BF16)
    q = jnp.dot(h, wq_ref[...], preferred_element_type=F32).astype(BF16)
    dh = q.shape[1] // heads
    scale = dh ** -0.5
    outs = []
    for hd in range(heads):
        cols = slice(hd * dh, (hd + 1) * dh)
        s = lax.dot_general(q[:, cols], mk_ref[0, :, cols].astype(BF16), (((1,), (1,)), ((), ())),
                            preferred_element_type=F32) * scale
        p = jnp.exp(s - jnp.max(s, axis=-1, keepdims=True))
        l = jnp.sum(p, axis=-1, keepdims=True)
        o = jnp.dot(p.astype(BF16), mv_ref[0, :, cols].astype(BF16), preferred_element_type=F32)
        outs.append((o / l).astype(BF16))
    o_all = jnp.concatenate(outs, axis=-1)
    o_ref[0] = x + jnp.dot(o_all, wo_ref[...], preferred_element_type=F32)


def _xattn(x, g, wq, mk, mv, wo, heads, *, name, tm_pref=512):
    b, t, d = x.shape
    nm, mw = mk.shape[1], mk.shape[2]
    tm = _pick_tile(t, tm_pref, SUBLANES)
    x_spec = pl.BlockSpec((1, tm, d), lambda bi, ti: (bi, ti, 0))
    mem_spec = pl.BlockSpec((1, nm, mw), lambda bi, ti: (bi, 0, 0))
    est = 4 * tm * d * 4 + 4 * d * mw * 2 + 4 * nm * mw * 4 + 3 * tm * d * 4
    return pl.pallas_call(
        functools.partial(_xattn_kernel, heads=heads),
        grid=(b, t // tm),
        in_specs=[x_spec, pl.BlockSpec((1, d), lambda bi, ti: (0, 0)),
                  pl.BlockSpec((d, mw), lambda bi, ti: (0, 0)), mem_spec, mem_spec,
                  pl.BlockSpec((mw, d), lambda bi, ti: (0, 0))],
        out_specs=x_spec,
        out_shape=jax.ShapeDtypeStruct((b, t, d), F32),
        compiler_params=_params(("parallel", "parallel"), est),
        name=name,
    )(x, g.reshape(1, d), wq, mk, mv, wo)


def _ffn_kernel(x_ref, g_ref, wg_ref, wu_ref, wo_ref, gf_ref, o_ref, h_ref, *, final_norm):
    j = pl.program_id(1)

    @pl.when(j == 0)
    def _():
        x = x_ref[...]
        h_ref[...] = _rms(x, g_ref[...]).astype(BF16)
        o_ref[...] = x

    h = h_ref[...]
    gate = jnp.dot(h, wg_ref[...], preferred_element_type=F32)
    up = jnp.dot(h, wu_ref[...], preferred_element_type=F32)
    act = (jax.nn.silu(gate) * up).astype(BF16)
    o_ref[...] += jnp.dot(act, wo_ref[...], preferred_element_type=F32)

    if final_norm:
        @pl.when(j == pl.num_programs(1) - 1)
        def _():
            o_ref[...] = _rms(o_ref[...], gf_ref[...])


def _ffn(x, g, w_in, w_out, g_final, *, final_norm, name, tm_pref=512, tf_pref=512):
    m, d = x.shape
    f = w_out.shape[0]
    tm = _pick_tile(m, tm_pref, SUBLANES)
    tf = _pick_tile(f, tf_pref, LANES)
    nf = f // tf
    x_spec = pl.BlockSpec((tm, d), lambda i, j: (i, 0))
    vec_spec = pl.BlockSpec((1, d), lambda i, j: (0, 0))
    est = 4 * tm * d * 4 + tm * d * 2 + 2 * 3 * d * tf * 2 + 4 * tm * tf * 4
    return pl.pallas_call(
        functools.partial(_ffn_kernel, final_norm=final_norm),
        grid=(m // tm, nf),
        in_specs=[x_spec, vec_spec,
                  pl.BlockSpec((d, tf), lambda i, j: (0, j)),
                  pl.BlockSpec((d, tf), lambda i, j: (0, nf + j)),
                  pl.BlockSpec((tf, d), lambda i, j: (j, 0)), vec_spec],
        out_specs=x_spec,
        out_shape=jax.ShapeDtypeStruct((m, d), F32),
        scratch_shapes=[pltpu.VMEM((tm, d), BF16)],
        compiler_params=_params(("parallel", "arbitrary"), est),
        name=name,
    )(x, g.reshape(1, d), w_in, w_in, w_out, g_final.reshape(1, d))


def kernel(x_prompt, x_sample, mem_prompt, cache_fox_k, cache_fox_v, cache_fox_logf, cache_mem_k, cache_mem_v, state_lru_h, state_lru_conv, norm_mix, norm_mem, norm_xattn, norm_ffn, norm_final, fox_w_in, fox_b_f, fox_w_out, lru_w_in, lru_conv_w, lru_conv_b, lru_w_ga, lru_b_a, lru_w_gx, lru_b_x, lru_lambda, lru_w_out, xattn_w_q, xattn_w_kv, xattn_w_o, ffn_w_in, ffn_w_out):
    b, s_len, d = x_prompt.shape
    bd, t_dec, _ = x_sample.shape
    depth = norm_mix.shape[0]
    heads = fox_b_f.shape[1]
    past = cache_fox_k.shape[2]
    n_mem, mem_heads = cache_mem_k.shape[2], cache_mem_k.shape[3]
    mem_w = mem_heads * cache_mem_k.shape[4]
    taps = lru_conv_w.shape[1]
    assert taps - 1 <= SUBLANES <= min(s_len, t_dec) and heads <= LANES

    xp = x_prompt.reshape(b * s_len, d)
    xs = x_sample.reshape(bd * t_dec, d)
    mem = mem_prompt.reshape(b * n_mem, d)
    pk, pv, plf, pmk, pmv, ph, pc = [], [], [], [], [], [], []
    sk, sv, slf, sh, sc = [], [], [], [], []

    for i in range(depth):
        j = i // 2
        if i % 2 == 0:
            w_in = fox_w_in[j]
            w_qkv = w_in[:, :3 * d].astype(BF16)
            w_f = jnp.pad(w_in[:, 3 * d:], ((0, 0), (0, LANES - heads))).astype(BF16)
            b_f = jnp.pad(fox_b_f[j], (0, LANES - heads)).reshape(1, LANES)
            w_out = fox_w_out[j].astype(BF16)
            proj = functools.partial(_norm_proj, g=norm_mix[i], w=w_qkv, acts=[_identity] * 3,
                                     dtypes=[BF16, F32, F32], extra=(w_f, b_f, LANES, _log_sigmoid))

            qp, kp, vp, lfp = proj(xp, name=f"fox_proj_p{i}")
            c_p = _cumsum_rows(lfp.reshape(b, s_len, LANES), name=f"fox_cumsum_p{i}")
            attn = _fox_attn(qp.reshape(b, s_len, d), kp.reshape(b, s_len, d), vp.reshape(b, s_len, d), c_p, heads,
                             name=f"fox_attn_p{i}")
            xp = _matmul_res(attn.reshape(b * s_len, d), w_out, xp, name=f"fox_out_p{i}")
            pk.append(kp.reshape(b, s_len, heads, d // heads))
            pv.append(vp.reshape(b, s_len, heads, d // heads))
            plf.append(lfp[:, :heads].reshape(b, s_len, heads))

            qs, ks, vs, lfs = proj(xs, name=f"fox_proj_s{i}")
            lfs = lfs[:, :heads]
            lf_all = jnp.concatenate([cache_fox_logf[j], lfs.reshape(bd, t_dec, heads)], axis=1)
            total = past + t_dec
            padded = -(-total // LANES) * LANES
            lf_t = jnp.pad(lf_all.transpose(0, 2, 1).reshape(bd * heads, total), ((0, 0), (0, padded - total)))
            c_s = _cumsum_lanes(lf_t, name=f"fox_cumsum_s{i}")
            attn = _fox_dec_attn(qs.reshape(bd, t_dec, d), ks.reshape(bd, t_dec, d), vs.reshape(bd, t_dec, d),
                                 cache_fox_k[j].reshape(bd, past, d), cache_fox_v[j].reshape(bd, past, d),
                                 c_s[:, :past], c_s[:, past:total], heads, name=f"fox_attn_s{i}")
            xs = _matmul_res(attn.reshape(bd * t_dec, d), w_out, xs, name=f"fox_out_s{i}")
            sk.append(ks.reshape(bd, t_dec, heads, d // heads))
            sv.append(vs.reshape(bd, t_dec, heads, d // heads))
            slf.append(lfs.reshape(bd, t_dec, heads))
        else:
            w_in = lru_w_in[j].astype(BF16)
            width = w_in.shape[1] // 2
            w_ga, w_gx = lru_w_ga[j].astype(BF16), lru_w_gx[j].astype(BF16)
            w_out = lru_w_out[j].astype(BF16)
            proj = functools.partial(_norm_proj, g=norm_mix[i], w=w_in, acts=[jax.nn.gelu, _identity], dtypes=[BF16, F32])
            scan = functools.partial(_lru_scan, conv_w=lru_conv_w[j], conv_b=lru_conv_b[j], w_ga=w_ga, b_a=lru_b_a[j],
                                     w_gx=w_gx, b_x=lru_b_x[j], lam=lru_lambda[j])

            gate, u = proj(xp, name=f"lru_proj_p{i}")
            y, h8, u8 = scan(u.reshape(b, s_len, width), gate.reshape(b, s_len, width),
                             h0=jnp.zeros((b, 1, width), F32), buf0=jnp.zeros((b, SUBLANES, width), F32),
                             name=f"lru_scan_p{i}")
            xp = _matmul_res(y.reshape(b * s_len, width), w_out, xp, name=f"lru_out_p{i}")
            ph.append(h8[:, SUBLANES - 1])
            pc.append(u8[:, SUBLANES - (taps - 1):])

            gate, u = proj(xs, name=f"lru_proj_s{i}")
            buf0 = jnp.pad(state_lru_conv[j], ((0, 0), (SUBLANES - (taps - 1), 0), (0, 0)))
            y, h8, u8 = scan(u.reshape(bd, t_dec, width), gate.reshape(bd, t_dec, width),
                             h0=state_lru_h[j].reshape(bd, 1, width), buf0=buf0, name=f"lru_scan_s{i}")
            xs = _matmul_res(y.reshape(bd * t_dec, width), w_out, xs, name=f"lru_out_s{i}")
            sh.append(h8[:, SUBLANES - 1])
            sc.append(u8[:, SUBLANES - (taps - 1):])

        mk, mv = _norm_proj(mem, norm_mem[i], xattn_w_kv[i].astype(BF16), [_identity] * 2, [F32, F32], name=f"mem_kv{i}")
        pmk.append(mk.reshape(b, n_mem, mem_heads, mem_w // mem_heads))
        pmv.append(mv.reshape(b, n_mem, mem_heads, mem_w // mem_heads))
        wq, wo = xattn_w_q[i].astype(BF16), xattn_w_o[i].astype(BF16)
        xp = _xattn(xp.reshape(b, s_len, d), norm_xattn[i], wq, mk.reshape(b, n_mem, mem_w), mv.reshape(b, n_mem, mem_w),
                    wo, mem_heads, name=f"xattn_p{i}").reshape(b * s_len, d)
        xs = _xattn(xs.reshape(bd, t_dec, d), norm_xattn[i], wq, cache_mem_k[i].reshape(bd, n_mem, mem_w),
                    cache_mem_v[i].reshape(bd, n_mem, mem_w), wo, mem_heads, name=f"xattn_s{i}").reshape(bd * t_dec, d)

        last = i == depth - 1
        w_in, w_out = ffn_w_in[i].astype(BF16), ffn_w_out[i].astype(BF16)
        xp = _ffn(xp, norm_ffn[i], w_in, w_out, norm_final, final_norm=last, name=f"ffn_p{i}")
        xs = _ffn(xs, norm_ffn[i], w_in, w_out, norm_final, final_norm=last, name=f"ffn_s{i}")

    return (xp.reshape(b, s_len, d), xs.reshape(bd, t_dec, d),
            jnp.stack(pk), jnp.stack(pv), jnp.stack(plf), jnp.stack(pmk), jnp.stack(pmv),
            jnp.stack(ph), jnp.stack(pc),
            jnp.stack(sk), jnp.stack(sv), jnp.stack(slf), jnp.stack(sh), jnp.stack(sc))
```

```python
import functools

import jax
import jax.numpy as jnp
from jax import lax
from jax.experimental import pallas as pl
from jax.experimental.pallas import tpu as pltpu

F32 = jnp.float32
BF16 = jnp.bfloat16

EPS = 1e-6
LOG2E = 1.4426950408889634
TINY = 1e-30
LRU_C = 8.0
LANES = 128
SUBLANES = 8
ONES_ROWS = 16
V7X_VMEM_BUDGET = 56 * 1024 * 1024


def _pick_tile(n, pref, align):
    if n <= pref:
        return n
    t = (pref // align) * align
    while t >= align:
        if n % t == 0:
            return t
        t -= align
    raise ValueError(f"no {align}-aligned tile of {n} below {pref}")


def _params(semantics, vmem_estimate):
    limit = int(min(V7X_VMEM_BUDGET, max(2 * vmem_estimate, 16 * 1024 * 1024)))
    return pltpu.CompilerParams(dimension_semantics=semantics, vmem_limit_bytes=limit)


def _rms(xf, g):
    return xf * lax.rsqrt(jnp.mean(xf * xf, axis=-1, keepdims=True) + EPS) * g


def _identity(x):
    return x


def _softplus(x):
    return jnp.maximum(x, 0.0) + jnp.log1p(jnp.exp(-jnp.abs(x)))


def _log_sigmoid(x):
    return -_softplus(-x)


def _norm_proj_kernel(*refs, acts, extra_act):
    n_sec = len(acts)
    x_ref, g_ref = refs[0], refs[1]
    w_refs = refs[2:2 + n_sec]
    pos = 2 + n_sec
    if extra_act is not None:
        we_ref, be_ref = refs[pos], refs[pos + 1]
        pos += 2
    o_refs = refs[pos:pos + n_sec]
    pos += n_sec
    if extra_act is not None:
        e_ref = refs[pos]
        pos += 1
    h_ref = refs[pos]

    @pl.when(pl.program_id(1) == 0)
    def _():
        h = _rms(x_ref[...], g_ref[...]).astype(BF16)
        h_ref[...] = h
        if extra_act is not None:
            z = jnp.dot(h, we_ref[...], preferred_element_type=F32) + be_ref[...]
            e_ref[...] = extra_act(z)[:, :e_ref.shape[1]]

    h = h_ref[...]
    for w_ref, o_ref, act in zip(w_refs, o_refs, acts):
        o_ref[...] = act(jnp.dot(h, w_ref[...], preferred_element_type=F32)).astype(o_ref.dtype)


def _norm_proj(x, g, w, layer, ns, acts, dtypes, *, name, extra=None, tm_pref=1024, tn_pref=512):
    m, d = x.shape
    n_sec = len(acts)
    tm = _pick_tile(m, tm_pref, SUBLANES)
    tn = _pick_tile(ns, tn_pref, LANES)
    tiles = ns // tn
    in_specs = [pl.BlockSpec((tm, d), lambda i, j: (i, 0)),
                pl.BlockSpec((1, d), lambda i, j: (0, 0))]
    in_specs += [pl.BlockSpec((None, d, tn), lambda i, j, s=s: (layer, 0, s * tiles + j)) for s in range(n_sec)]
    args = [x, g.reshape(1, d)] + [w] * n_sec
    out_specs = [pl.BlockSpec((tm, tn), lambda i, j: (i, j)) for _ in range(n_sec)]
    out_shape = [jax.ShapeDtypeStruct((m, ns), dt) for dt in dtypes]
    extra_act = None
    if extra is not None:
        w_e, b_e, n_e, extra_act = extra
        in_specs += [pl.BlockSpec(w_e.shape, lambda i, j: (0, 0)), pl.BlockSpec(b_e.shape, lambda i, j: (0, 0))]
        args += [w_e, b_e]
        out_specs.append(pl.BlockSpec((tm, n_e), lambda i, j: (i, 0)))
        out_shape.append(jax.ShapeDtypeStruct((m, n_e), F32))
    est = 2 * tm * d * 4 + tm * d * 2 + n_sec * 2 * (d * tn * 2 + tm * tn * 4) + 4 * tm * tn * 4
    return pl.pallas_call(
        functools.partial(_norm_proj_kernel, acts=tuple(acts), extra_act=extra_act),
        grid=(m // tm, tiles),
        in_specs=in_specs, out_specs=out_specs, out_shape=out_shape,
        scratch_shapes=[pltpu.VMEM((tm, d), BF16)],
        compiler_params=_params(("parallel", "arbitrary"), est),
        name=name,
    )(*args)


def _matmul_res_kernel(a_ref, w_ref, r_ref, o_ref, *, tn):
    a = a_ref[...]
    for n0 in range(0, o_ref.shape[1], tn):
        o_ref[:, n0:n0 + tn] = r_ref[:, n0:n0 + tn] + jnp.dot(a, w_ref[:, n0:n0 + tn], preferred_element_type=F32)


def _matmul_res(a, w, layer, res, *, name, tm_pref=512, tn_pref=512):
    m, k = a.shape
    n = w.shape[2]
    tm = _pick_tile(m, tm_pref, SUBLANES)
    tn = _pick_tile(n, tn_pref, LANES)
    est = 2 * (tm * k * 2 + k * n * 2 + 2 * tm * n * 4) + 2 * tm * tn * 4
    return pl.pallas_call(
        functools.partial(_matmul_res_kernel, tn=tn),
        grid=(m // tm,),
        in_specs=[pl.BlockSpec((tm, k), lambda i: (i, 0)),
                  pl.BlockSpec((None, k, n), lambda i: (layer, 0, 0)),
                  pl.BlockSpec((tm, n), lambda i: (i, 0))],
        out_specs=pl.BlockSpec((tm, n), lambda i: (i, 0)),
        out_shape=jax.ShapeDtypeStruct((m, n), F32),
        compiler_params=_params(("parallel",), est),
        name=name,
    )(a, w, res)


def _cumsum_kernel(x_ref, o_ref, carry_ref):
    @pl.when(pl.program_id(0) == 0)
    def _():
        carry_ref[...] = jnp.zeros_like(carry_ref)

    x = x_ref[...]
    n = x.shape[1]
    upper = (lax.broadcasted_iota(jnp.int32, (n, n), 0) <= lax.broadcasted_iota(jnp.int32, (n, n), 1)).astype(BF16)
    hi = x.astype(BF16)
    rest = x - hi.astype(F32)
    mid = rest.astype(BF16)
    lo = (rest - mid.astype(F32)).astype(BF16)
    local = (jnp.dot(hi, upper, preferred_element_type=F32) + jnp.dot(mid, upper, preferred_element_type=F32)
             + jnp.dot(lo, upper, preferred_element_type=F32))
    out = local + carry_ref[...]
    o_ref[...] = out
    carry_ref[...] = jnp.broadcast_to(out[:, n - 1:n], out.shape)


def _cumsum_lanes(x, *, name):
    r, l = x.shape
    return pl.pallas_call(
        _cumsum_kernel,
        grid=(l // LANES,),
        in_specs=[pl.BlockSpec((r, LANES), lambda t: (0, t))],
        out_specs=pl.BlockSpec((r, LANES), lambda t: (0, t)),
        out_shape=jax.ShapeDtypeStruct((r, l), F32),
        scratch_shapes=[pltpu.VMEM((r, LANES), F32)],
        compiler_params=_params(("arbitrary",), 8 * r * LANES * 4),
        name=name,
    )(x)


def _cumsum_rows_kernel(x_ref, o_ref, carry_ref):
    @pl.when(pl.program_id(1) == 0)
    def _():
        carry_ref[...] = jnp.zeros_like(carry_ref)

    x = x_ref[0]
    n = x.shape[0]
    lower = (lax.broadcasted_iota(jnp.int32, (n, n), 0) >= lax.broadcasted_iota(jnp.int32, (n, n), 1)).astype(BF16)
    hi = x.astype(BF16)
    rest = x - hi.astype(F32)
    mid = rest.astype(BF16)
    lo = (rest - mid.astype(F32)).astype(BF16)
    local = (jnp.dot(lower, hi, preferred_element_type=F32) + jnp.dot(lower, mid, preferred_element_type=F32)
             + jnp.dot(lower, lo, preferred_element_type=F32))
    out = local + carry_ref[...]
    o_ref[0] = out
    carry_ref[...] = out[n - 1:n, :]


def _cumsum_rows(x, *, name, tile_pref=256):
    b, s_len, w = x.shape
    tile = _pick_tile(s_len, tile_pref, SUBLANES)
    spec = pl.BlockSpec((1, tile, w), lambda bi, ti: (bi, ti, 0))
    return pl.pallas_call(
        _cumsum_rows_kernel,
        grid=(b, s_len // tile),
        in_specs=[spec], out_specs=spec,
        out_shape=jax.ShapeDtypeStruct((b, s_len, w), F32),
        scratch_shapes=[pltpu.VMEM((1, w), F32)],
        compiler_params=_params(("parallel", "arbitrary"), 8 * tile * w * 4 + 4 * tile * tile),
        name=name,
    )(x)


def _fox_attn_kernel(q_ref, k_ref, v_ref, c_ref, o_ref, kaug_ref, vt_ref, ta_ref, tb_ref, pa_ref, pb_ref,
                     acc_ref, m_ref, *, tq, heads, unroll):
    s_len, dh = q_ref.shape[1], q_ref.shape[2]
    nk = s_len // tq
    h = pl.program_id(1)

    @pl.when(h == 0)
    def _():
        head_lane = lax.broadcasted_iota(jnp.int32, c_ref.shape[1:], 1) < heads
        ckp = jnp.where(head_lane, c_ref[0] * (-LOG2E), 0.0)
        hi = ckp.astype(BF16).astype(F32)
        rest = ckp - hi
        mid = rest.astype(BF16).astype(F32)
        lo = (rest - mid).astype(BF16).astype(F32)
        aug = hi + pltpu.roll(mid, heads, axis=1) + pltpu.roll(lo, 2 * heads, axis=1)
        kaug_ref[:, dh:] = aug.astype(BF16)

    kaug_ref[:, :dh] = k_ref[0].astype(BF16)
    for kj in range(nk):
        vt_ref[kj, :dh] = v_ref[0, kj * tq:(kj + 1) * tq, :].T.astype(BF16)
        vt_ref[kj, dh:] = (lax.broadcasted_iota(jnp.int32, (ONES_ROWS, tq), 0) == 0).astype(BF16)

    lane = lax.broadcasted_iota(jnp.int32, (tq, LANES), 1)
    pick = ((lane == h) | (lane == heads + h) | (lane == 2 * heads + h)).astype(BF16)
    contract_last = (((1,), (1,)), ((), ()))
    t_refs, p_refs = (ta_ref, tb_ref), (pa_ref, pb_ref)

    m_ref[...] = jnp.full(m_ref.shape, -jnp.inf, F32)
    acc_ref[...] = jnp.zeros(acc_ref.shape, F32)

    def scores(pair, masked, t_ref):
        qa, ka = pair
        q = q_ref[0, pl.ds(pl.multiple_of(qa * tq, tq), tq), :]
        kblk = kaug_ref[pl.ds(pl.multiple_of(ka * tq, tq), tq), :]
        t = lax.dot_general(kblk, jnp.concatenate([q, pick], axis=1), contract_last, preferred_element_type=F32)
        if masked:
            causal = lax.broadcasted_iota(jnp.int32, (tq, tq), 0) <= lax.broadcasted_iota(jnp.int32, (tq, tq), 1)
            t = jnp.where(causal, t, -jnp.inf)
        t_ref[...] = t
        return jnp.max(t, axis=0, keepdims=True)

    def softmax_update(pair, t_ref, p_ref, m_blk):
        qa, _ = pair
        m_prev = m_ref[qa]
        m_new = jnp.maximum(m_prev, m_blk)
        alpha = jnp.exp2(m_prev - m_new)
        p_ref[...] = jnp.exp2(t_ref[...] - m_new).astype(BF16)
        m_ref[qa] = m_new
        return alpha

    def accumulate(pair, p_ref, alpha):
        qa, ka = pair
        acc_ref[qa] = alpha * acc_ref[qa] + jnp.dot(vt_ref[ka], p_ref[...], preferred_element_type=F32)

    def pipeline_step(e, pairs, masked, state):
        pair_a, pair_b, pair_c = pairs
        m_blk, alpha = state
        if pair_c is not None:
            accumulate(pair_c, p_refs[e % 2], alpha)
        if pair_b is not None:
            alpha = softmax_update(pair_b, t_refs[(e - 1) % 2], p_refs[(e - 1) % 2], m_blk)
        if pair_a is not None:
            m_blk = scores(pair_a, masked, t_refs[e % 2])
        return m_blk, alpha

    static_pairs = [(qa, qa) for qa in range(nk)] + [(qa, ka) for qa in range(nk) for ka in range(qa)]
    n_steps = len(static_pairs)
    loop_start = nk + nk % 2
    n_loop = max(n_steps - loop_start, 0) // unroll

    def static_step(e, state):
        pairs = [static_pairs[i] if 0 <= i < n_steps else None for i in (e, e - 1, e - 2)]
        return pipeline_step(e, pairs, e < nk, state)

    def advance(pair):
        qa, ka = pair
        wrap = ka + 1 == qa
        return jnp.where(wrap, qa + 1, qa), jnp.where(wrap, 0, ka + 1)

    def body(_, carry):
        state, pairs = carry
        for e in range(unroll):
            state = pipeline_step(e, pairs, False, state)
            pairs = (advance(pairs[0]), pairs[0], pairs[1])
        return state, pairs

    state = (jnp.zeros((1, tq), F32), jnp.zeros((1, tq), F32))
    for e in range(loop_start):
        state = static_step(e, state)
    if n_loop:
        first = tuple((jnp.int32(qa), jnp.int32(ka)) for qa, ka in (static_pairs[loop_start - i] for i in range(3)))
        state, _ = lax.fori_loop(0, n_loop, body, (state, first))
    for e in range(loop_start + unroll * n_loop, n_steps + 2):
        state = static_step(e, state)

    for qi in range(nk):
        out_t = acc_ref[qi, :dh] / acc_ref[qi, dh:dh + 1]
        o_ref[0, qi * tq:(qi + 1) * tq, :] = out_t.T.astype(o_ref.dtype)


def _fox_attn(q, k, v, c, heads, *, name, tq_pref=512, unroll=2):
    b, s_len, d = q.shape
    dh = d // heads
    assert unroll % 2 == 0 and 3 * heads <= LANES
    tq = _pick_tile(s_len, tq_pref, LANES)
    nk = s_len // tq
    qkv_spec = pl.BlockSpec((1, s_len, dh), lambda bi, hi: (bi, 0, hi))
    est = (2 * s_len * dh * (2 + 4 + 4 + 2) + 2 * s_len * LANES * 4 + s_len * (2 * dh + LANES) * 2
           + s_len * (dh + ONES_ROWS + 1) * 4 + 3 * tq * tq * 4 + 8 * tq * tq * 4)
    return pl.pallas_call(
        functools.partial(_fox_attn_kernel, tq=tq, heads=heads, unroll=unroll),
        grid=(b, heads),
        in_specs=[qkv_spec, qkv_spec, qkv_spec,
                  pl.BlockSpec((1, s_len, LANES), lambda bi, hi: (bi, 0, 0))],
        out_specs=qkv_spec,
        out_shape=jax.ShapeDtypeStruct((b, s_len, d), BF16),
        scratch_shapes=[pltpu.VMEM((s_len, dh + LANES), BF16), pltpu.VMEM((nk, dh + ONES_ROWS, tq), BF16),
                        pltpu.VMEM((tq, tq), F32), pltpu.VMEM((tq, tq), F32),
                        pltpu.VMEM((tq, tq), BF16), pltpu.VMEM((tq, tq), BF16),
                        pltpu.VMEM((nk, dh + ONES_ROWS, tq), F32), pltpu.VMEM((nk, 1, tq), F32)],
        compiler_params=_params(("parallel", "arbitrary"), est),
        name=name,
    )(q, k, v, c)


def _fox_dec_kernel(q_ref, kn_ref, vn_ref, kc_ref, vc_ref, ckc_ref, ckn_ref, o_ref, *, scale):
    q = q_ref[0]
    t = q.shape[0]
    contract_last = (((1,), (1,)), ((), ()))
    s_c = lax.dot_general(q, kc_ref[0].astype(BF16), contract_last, preferred_element_type=F32) * scale - ckc_ref[0]
    s_n = lax.dot_general(q, kn_ref[0].astype(BF16), contract_last, preferred_element_type=F32) * scale - ckn_ref[0]
    causal = lax.broadcasted_iota(jnp.int32, (t, t), 1) <= lax.broadcasted_iota(jnp.int32, (t, t), 0)
    s_n = jnp.where(causal, s_n, -jnp.inf)
    m = jnp.maximum(jnp.max(s_c, axis=-1, keepdims=True), jnp.max(s_n, axis=-1, keepdims=True))
    p_c = jnp.exp(s_c - m)
    p_n = jnp.exp(s_n - m)
    l = jnp.sum(p_c, axis=-1, keepdims=True) + jnp.sum(p_n, axis=-1, keepdims=True)
    acc = (jnp.dot(p_c.astype(BF16), vc_ref[0].astype(BF16), preferred_element_type=F32)
           + jnp.dot(p_n.astype(BF16), vn_ref[0].astype(BF16), preferred_element_type=F32))
    o_ref[0] = (acc / l).astype(o_ref.dtype)


def _fox_dec_attn(q, k_new, v_new, k_cache, v_cache, c_cache, c_new, heads, *, name):
    b, t, d = q.shape
    p = k_cache.shape[1]
    dh = d // heads
    new_spec = pl.BlockSpec((1, t, dh), lambda bi, hi: (bi, 0, hi))
    cache_spec = pl.BlockSpec((1, p, dh), lambda bi, hi: (bi, 0, hi))
    est = 2 * 2 * p * dh * 4 + 2 * p * dh * 2 + 6 * t * p * 4
    return pl.pallas_call(
        functools.partial(_fox_dec_kernel, scale=dh ** -0.5),
        grid=(b, heads),
        in_specs=[new_spec, new_spec, new_spec, cache_spec, cache_spec,
                  pl.BlockSpec((1, 1, p), lambda bi, hi: (bi * heads + hi, 0, 0)),
                  pl.BlockSpec((1, 1, t), lambda bi, hi: (bi * heads + hi, 0, 0))],
        out_specs=new_spec,
        out_shape=jax.ShapeDtypeStruct((b, t, d), BF16),
        compiler_params=_params(("parallel", "parallel"), est),
        name=name,
    )(q, k_new, v_new, k_cache, v_cache, c_cache.reshape(b * heads, 1, p), c_new.reshape(b * heads, 1, t))


def _lru_kernel(u_ref, gate_ref, cw_ref, cb_ref, wa_ref, ba_ref, wx_ref, bx_ref, lam_ref, h0_ref, buf0_ref,
                y_ref, hl_ref, ul_ref, h_scr, tail_scr):
    @pl.when(pl.program_id(2) == 0)
    def _():
        h_scr[...] = h0_ref[0]
        tail_scr[...] = buf0_ref[0]

    u = u_ref[0]
    tc, bw = u.shape
    taps = cw_ref.shape[0]
    tail = tail_scr[...]
    head_rows = lax.broadcasted_iota(jnp.int32, (SUBLANES, bw), 0)

    def delayed(d):
        if d == 0:
            return u
        ru = pltpu.roll(u, d, axis=0)
        head = jnp.where(head_rows < d, pltpu.roll(tail, d, axis=0), ru[:SUBLANES])
        return head if tc == SUBLANES else jnp.concatenate([head, ru[SUBLANES:]], axis=0)

    conv = delayed(taps - 1) * cw_ref[0:1, :]
    for k in range(1, taps):
        conv = conv + delayed(taps - 1 - k) * cw_ref[k:k + 1, :]
    uc = cb_ref[...] + conv
    tail_scr[...] = u[tc - SUBLANES:, :]
    ul_ref[0] = u[tc - SUBLANES:, :]

    ucb = uc.astype(BF16)
    r = jax.nn.sigmoid(jnp.dot(ucb, wa_ref[0], preferred_element_type=F32) + ba_ref[...])
    ig = jax.nn.sigmoid(jnp.dot(ucb, wx_ref[0], preferred_element_type=F32) + bx_ref[...])
    log_a = (-LRU_C * _softplus(-lam_ref[...])) * r
    a = jnp.exp(log_a)
    th = jnp.tanh(log_a)
    decay = (-2.0 * th) / (1.0 - th)
    b = (decay * lax.rsqrt(jnp.maximum(decay, TINY))) * (ig * uc)

    groups = tc // SUBLANES
    a3 = a.reshape(groups, SUBLANES, bw)
    b3 = b.reshape(groups, SUBLANES, bw)
    sub = lax.broadcasted_iota(jnp.int32, (groups, SUBLANES, bw), 1)
    d = 1
    while d < SUBLANES:
        valid = sub >= d
        b3 = jnp.where(valid, a3 * pltpu.roll(b3, d, axis=1) + b3, b3)
        a3 = jnp.where(valid, a3 * pltpu.roll(a3, d, axis=1), a3)
        d *= 2
    h = h_scr[...]
    hs = []
    for gi in range(groups):
        hg = a3[gi] * h + b3[gi]
        hs.append(hg)
        h = hg[SUBLANES - 1:SUBLANES, :]
    h_scr[...] = h
    hl_ref[0] = hs[-1]
    hs = hs[0] if groups == 1 else jnp.concatenate(hs, axis=0)
    y_ref[0] = (hs * gate_ref[0].astype(F32)).astype(y_ref.dtype)


def _lru_scan(u, gate, conv_w, conv_b, w_ga, b_a, w_gx, b_x, lam, layer, h0, buf0, *, name, tc_pref=512):
    b, t, w = u.shape
    _, nb, bw, _ = w_ga.shape
    taps = conv_w.shape[0]
    tc = _pick_tile(t, tc_pref, SUBLANES)
    seq_spec = pl.BlockSpec((1, tc, bw), lambda bi, ni, ti: (bi, ti, ni))
    row_spec = pl.BlockSpec((1, bw), lambda bi, ni, ti: (0, ni))
    blk_spec = pl.BlockSpec((None, 1, bw, bw), lambda bi, ni, ti: (layer, ni, 0, 0))
    last_spec = pl.BlockSpec((1, SUBLANES, bw), lambda bi, ni, ti: (bi, 0, ni))
    est = 2 * tc * bw * (4 + 2 + 2) + 4 * bw * bw * 2 + 24 * tc * bw * 4
    return pl.pallas_call(
        _lru_kernel,
        grid=(b, nb, t // tc),
        in_specs=[seq_spec, seq_spec,
                  pl.BlockSpec((taps, bw), lambda bi, ni, ti: (0, ni)), row_spec,
                  blk_spec, row_spec, blk_spec, row_spec, row_spec,
                  pl.BlockSpec((1, 1, bw), lambda bi, ni, ti: (bi, 0, ni)), last_spec],
        out_specs=[seq_spec, last_spec, last_spec],
        out_shape=[jax.ShapeDtypeStruct((b, t, w), BF16),
                   jax.ShapeDtypeStruct((b, SUBLANES, w), F32),
                   jax.ShapeDtypeStruct((b, SUBLANES, w), F32)],
        scratch_shapes=[pltpu.VMEM((1, bw), F32), pltpu.VMEM((SUBLANES, bw), F32)],
        compiler_params=_params(("parallel", "parallel", "arbitrary"), est),
        name=name,
    )(u, gate, conv_w, conv_b.reshape(1, w), w_ga, b_a.reshape(1, w), w_gx, b_x.reshape(1, w),
      lam.reshape(1, w), h0, buf0)


def _xattn_kernel(x_ref, g_ref, wq_ref, mk_ref, mv_ref, wo_ref, o_ref, *, heads):
    x = x_ref[0]
    h = _rms(x, g_ref[...]).astype(BF16)
    q = jnp.dot(h, wq_ref[...], preferred_element_type=F32).astype(BF16)
    dh = q.shape[1] // heads
    scale = dh ** -0.5
    outs = []
    for hd in range(heads):
        cols = slice(hd * dh, (hd + 1) * dh)
        s = lax.dot_general(q[:, cols], mk_ref[0, :, cols].astype(BF16), (((1,), (1,)), ((), ())),
                            preferred_element_type=F32) * scale
        p = jnp.exp(s - jnp.max(s, axis=-1, keepdims=True))
        l = jnp.sum(p, axis=-1, keepdims=True)
        o = jnp.dot(p.astype(BF16), mv_ref[0, :, cols].astype(BF16), preferred_element_type=F32)
        outs.append((o / l).astype(BF16))
    o_all = jnp.concatenate(outs, axis=-1)
    o_ref[0] = x + jnp.dot(o_all, wo_ref[...], preferred_element_type=F32)


def _xattn(x, g, wq, mk, mv, wo, layer, heads, *, name, tm_pref=1024):
    b, t, d = x.shape
    nm, mw = mk.shape[1], mk.shape[2]
    tm = _pick_tile(t, tm_pref, SUBLANES)
    x_spec = pl.BlockSpec((1, tm, d), lambda bi, ti: (bi, ti, 0))
    mem_spec = pl.BlockSpec((1, nm, mw), lambda bi, ti: (bi, 0, 0))
    est = 4 * tm * d * 4 + 4 * d * mw * 2 + 4 * nm * mw * 4 + 3 * tm * d * 4
    return pl.pallas_call(
        functools.partial(_xattn_kernel, heads=heads),
        grid=(b, t // tm),
        in_specs=[x_spec, pl.BlockSpec((1, d), lambda bi, ti: (0, 0)),
                  pl.BlockSpec((None, d, mw), lambda bi, ti: (layer, 0, 0)), mem_spec, mem_spec,
                  pl.BlockSpec((None, mw, d), lambda bi, ti: (layer, 0, 0))],
        out_specs=x_spec,
        out_shape=jax.ShapeDtypeStruct((b, t, d), F32),
        compiler_params=_params(("parallel", "parallel"), est),
        name=name,
    )(x, g.reshape(1, d), wq, mk, mv, wo)


def _ffn_kernel(x_ref, g_ref, wg_ref, wu_ref, wo_ref, gf_ref, o_ref, h_ref, *, final_norm):
    j = pl.program_id(1)

    @pl.when(j == 0)
    def _():
        x = x_ref[...]
        h_ref[...] = _rms(x, g_ref[...]).astype(BF16)
        o_ref[...] = x

    h = h_ref[...]
    gate = jnp.dot(h, wg_ref[...], preferred_element_type=F32)
    up = jnp.dot(h, wu_ref[...], preferred_element_type=F32)
    act = (jax.nn.silu(gate) * up).astype(BF16)
    o_ref[...] += jnp.dot(act, wo_ref[...], preferred_element_type=F32)

    if final_norm:
        @pl.when(j == pl.num_programs(1) - 1)
        def _():
            o_ref[...] = _rms(o_ref[...], gf_ref[...])


def _ffn(x, g, w_in, w_out, layer, g_final, *, final_norm, name, tm_pref=1024, tf_pref=512):
    m, d = x.shape
    f = w_out.shape[1]
    tm = _pick_tile(m, tm_pref, SUBLANES)
    tf = _pick_tile(f, tf_pref, LANES)
    nf = f // tf
    x_spec = pl.BlockSpec((tm, d), lambda i, j: (i, 0))
    vec_spec = pl.BlockSpec((1, d), lambda i, j: (0, 0))
    est = 4 * tm * d * 4 + tm * d * 2 + 2 * 3 * d * tf * 2 + 4 * tm * tf * 4
    return pl.pallas_call(
        functools.partial(_ffn_kernel, final_norm=final_norm),
        grid=(m // tm, nf),
        in_specs=[x_spec, vec_spec,
                  pl.BlockSpec((None, d, tf), lambda i, j: (layer, 0, j)),
                  pl.BlockSpec((None, d, tf), lambda i, j: (layer, 0, nf + j)),
                  pl.BlockSpec((None, tf, d), lambda i, j: (layer, j, 0)), vec_spec],
        out_specs=x_spec,
        out_shape=jax.ShapeDtypeStruct((m, d), F32),
        scratch_shapes=[pltpu.VMEM((tm, d), BF16)],
        compiler_params=_params(("parallel", "arbitrary"), est),
        name=name,
    )(x, g.reshape(1, d), w_in, w_in, w_out, g_final.reshape(1, d))


def kernel(x_prompt, x_sample, mem_prompt, cache_fox_k, cache_fox_v, cache_fox_logf, cache_mem_k, cache_mem_v, state_lru_h, state_lru_conv, norm_mix, norm_mem, norm_xattn, norm_ffn, norm_final, fox_w_in, fox_b_f, fox_w_out, lru_w_in, lru_conv_w, lru_conv_b, lru_w_ga, lru_b_a, lru_w_gx, lru_b_x, lru_lambda, lru_w_out, xattn_w_q, xattn_w_kv, xattn_w_o, ffn_w_in, ffn_w_out):
    b, s_len, d = x_prompt.shape
    bd, t_dec, _ = x_sample.shape
    depth = norm_mix.shape[0]
    heads = fox_b_f.shape[1]
    past = cache_fox_k.shape[2]
    n_mem, mem_heads = cache_mem_k.shape[2], cache_mem_k.shape[3]
    mem_w = mem_heads * cache_mem_k.shape[4]
    taps = lru_conv_w.shape[1]
    assert taps - 1 <= SUBLANES <= min(s_len, t_dec) and heads <= LANES

    xp = x_prompt.reshape(b * s_len, d)
    xs = x_sample.reshape(bd * t_dec, d)
    mem = mem_prompt.reshape(b * n_mem, d)
    pk, pv, plf, pmk, pmv, ph, pc = [], [], [], [], [], [], []
    sk, sv, slf, sh, sc = [], [], [], [], []

    fox_in, fox_out = fox_w_in.astype(BF16), fox_w_out.astype(BF16)
    lru_in, lru_out = lru_w_in.astype(BF16), lru_w_out.astype(BF16)
    lru_ga, lru_gx = lru_w_ga.astype(BF16), lru_w_gx.astype(BF16)
    xq, xkv, xo = xattn_w_q.astype(BF16), xattn_w_kv.astype(BF16), xattn_w_o.astype(BF16)
    ffn_in, ffn_out = ffn_w_in.astype(BF16), ffn_w_out.astype(BF16)

    for i in range(depth):
        j = i // 2
        if i % 2 == 0:
            w_f = jnp.pad(fox_in[j, :, 3 * d:], ((0, 0), (0, LANES - heads)))
            b_f = jnp.pad(fox_b_f[j], (0, LANES - heads)).reshape(1, LANES)
            q_scale = (d // heads) ** -0.5 * LOG2E
            proj = functools.partial(_norm_proj, g=norm_mix[i], w=fox_in, layer=j, ns=d, dtypes=[BF16, F32, F32],
                                     extra=(w_f, b_f, LANES, _log_sigmoid))
            proj_p = functools.partial(proj, acts=[lambda q: q * q_scale, _identity, _identity])
            proj_s = functools.partial(proj, acts=[_identity] * 3)

            qp, kp, vp, lfp = proj_p(xp, name=f"fox_proj_p{i}")
            c_p = _cumsum_rows(lfp.reshape(b, s_len, LANES), name=f"fox_cumsum_p{i}")
            attn = _fox_attn(qp.reshape(b, s_len, d), kp.reshape(b, s_len, d), vp.reshape(b, s_len, d), c_p, heads,
                             name=f"fox_attn_p{i}")
            xp = _matmul_res(attn.reshape(b * s_len, d), fox_out, j, xp, name=f"fox_out_p{i}")
            pk.append(kp.reshape(b, s_len, heads, d // heads))
            pv.append(vp.reshape(b, s_len, heads, d // heads))
            plf.append(lfp[:, :heads].reshape(b, s_len, heads))

            qs, ks, vs, lfs = proj_s(xs, name=f"fox_proj_s{i}")
            lfs = lfs[:, :heads]
            lf_all = jnp.concatenate([cache_fox_logf[j], lfs.reshape(bd, t_dec, heads)], axis=1)
            total = past + t_dec
            padded = -(-total // LANES) * LANES
            lf_t = jnp.pad(lf_all.transpose(0, 2, 1).reshape(bd * heads, total), ((0, 0), (0, padded - total)))
            c_s = _cumsum_lanes(lf_t, name=f"fox_cumsum_s{i}")
            attn = _fox_dec_attn(qs.reshape(bd, t_dec, d), ks.reshape(bd, t_dec, d), vs.reshape(bd, t_dec, d),
                                 cache_fox_k[j].reshape(bd, past, d), cache_fox_v[j].reshape(bd, past, d),
                                 c_s[:, :past], c_s[:, past:total], heads, name=f"fox_attn_s{i}")
            xs = _matmul_res(attn.reshape(bd * t_dec, d), fox_out, j, xs, name=f"fox_out_s{i}")
            sk.append(ks.reshape(bd, t_dec, heads, d // heads))
            sv.append(vs.reshape(bd, t_dec, heads, d // heads))
            slf.append(lfs.reshape(bd, t_dec, heads))
        else:
            width = lru_in.shape[2] // 2
            proj = functools.partial(_norm_proj, g=norm_mix[i], w=lru_in, layer=j, ns=width,
                                     acts=[jax.nn.gelu, _identity], dtypes=[BF16, F32])
            scan = functools.partial(_lru_scan, conv_w=lru_conv_w[j], conv_b=lru_conv_b[j], w_ga=lru_ga, b_a=lru_b_a[j],
                                     w_gx=lru_gx, b_x=lru_b_x[j], lam=lru_lambda[j], layer=j)

            gate, u = proj(xp, name=f"lru_proj_p{i}")
            y, h8, u8 = scan(u.reshape(b, s_len, width), gate.reshape(b, s_len, width),
                             h0=jnp.zeros((b, 1, width), F32), buf0=jnp.zeros((b, SUBLANES, width), F32),
                             name=f"lru_scan_p{i}")
            xp = _matmul_res(y.reshape(b * s_len, width), lru_out, j, xp, name=f"lru_out_p{i}")
            ph.append(h8[:, SUBLANES - 1])
            pc.append(u8[:, SUBLANES - (taps - 1):])

            gate, u = proj(xs, name=f"lru_proj_s{i}")
            buf0 = jnp.pad(state_lru_conv[j], ((0, 0), (SUBLANES - (taps - 1), 0), (0, 0)))
            y, h8, u8 = scan(u.reshape(bd, t_dec, width), gate.reshape(bd, t_dec, width),
                             h0=state_lru_h[j].reshape(bd, 1, width), buf0=buf0, name=f"lru_scan_s{i}")
            xs = _matmul_res(y.reshape(bd * t_dec, width), lru_out, j, xs, name=f"lru_out_s{i}")
            sh.append(h8[:, SUBLANES - 1])
            sc.append(u8[:, SUBLANES - (taps - 1):])

        mk, mv = _norm_proj(mem, norm_mem[i], xkv, i, mem_w, [_identity] * 2, [F32, F32], name=f"mem_kv{i}")
        pmk.append(mk.reshape(b, n_mem, mem_heads, mem_w // mem_heads))
        pmv.append(mv.reshape(b, n_mem, mem_heads, mem_w // mem_heads))
        xp = _xattn(xp.reshape(b, s_len, d), norm_xattn[i], xq, mk.reshape(b, n_mem, mem_w), mv.reshape(b, n_mem, mem_w),
                    xo, i, mem_heads, name=f"xattn_p{i}").reshape(b * s_len, d)
        xs = _xattn(xs.reshape(bd, t_dec, d), norm_xattn[i], xq, cache_mem_k[i].reshape(bd, n_mem, mem_w),
                    cache_mem_v[i].reshape(bd, n_mem, mem_w), xo, i, mem_heads, name=f"xattn_s{i}").reshape(bd * t_dec, d)

        last = i == depth - 1
        xp = _ffn(xp, norm_ffn[i], ffn_in, ffn_out, i, norm_final, final_norm=last, name=f"ffn_p{i}")
        xs = _ffn(xs, norm_ffn[i], ffn_in, ffn_out, i, norm_final, final_norm=last, name=f"ffn_s{i}")

    return (xp.reshape(b, s_len, d), xs.reshape(bd, t_dec, d),
            jnp.stack(pk), jnp.stack(pv), jnp.stack(plf), jnp.stack(pmk), jnp.stack(pmv),
            jnp.stack(ph), jnp.stack(pc),
            jnp.stack(sk), jnp.stack(sv), jnp.stack(slf), jnp.stack(sh), jnp.stack(sc))
```

```python
import functools

import jax
import jax.numpy as jnp
from jax import lax
from jax.experimental import pallas as pl
from jax.experimental.pallas import tpu as pltpu

F32 = jnp.float32
BF16 = jnp.bfloat16

EPS = 1e-6
LOG2E = 1.4426950408889634
TINY = 1e-30
LRU_C = 8.0
LANES = 128
SUBLANES = 8
ONES_ROWS = 16
NORM_CHUNKS = 4
V7X_VMEM_BUDGET = 56 * 1024 * 1024


def _pick_tile(n, pref, align):
    if n <= pref:
        return n
    t = (pref // align) * align
    while t >= align:
        if n % t == 0:
            return t
        t -= align
    raise ValueError(f"no {align}-aligned tile of {n} below {pref}")


def _params(semantics, vmem_estimate):
    limit = int(min(V7X_VMEM_BUDGET, max(2 * vmem_estimate, 16 * 1024 * 1024)))
    return pltpu.CompilerParams(dimension_semantics=semantics, vmem_limit_bytes=limit)


def _rms(xf, g):
    return xf * lax.rsqrt(jnp.mean(xf * xf, axis=-1, keepdims=True) + EPS) * g


def _identity(x):
    return x


def _row_chunks(rows):
    step = rows // NORM_CHUNKS if rows % (NORM_CHUNKS * 2 * SUBLANES) == 0 else rows
    return [slice(r0, r0 + step) for r0 in range(0, rows, step)]


def _softplus(x):
    return jnp.maximum(x, 0.0) + jnp.log1p(jnp.exp(-jnp.abs(x)))


def _log_sigmoid(x):
    return -_softplus(-x)


def _norm_proj_kernel(*refs, acts, extra_act):
    n_sec = len(acts)
    x_ref, g_ref = refs[0], refs[1]
    w_refs = refs[2:2 + n_sec]
    pos = 2 + n_sec
    if extra_act is not None:
        we_ref, be_ref = refs[pos], refs[pos + 1]
        pos += 2
    o_refs = refs[pos:pos + n_sec]
    pos += n_sec
    if extra_act is not None:
        e_ref = refs[pos]
        pos += 1
    h_ref = refs[pos]

    def project(h, rows):
        for w_ref, o_ref, act in zip(w_refs, o_refs, acts):
            o_ref[rows, :] = act(jnp.dot(h, w_ref[...], preferred_element_type=F32)).astype(o_ref.dtype)

    first = pl.program_id(1) == 0

    @pl.when(first)
    def _():
        for rows in _row_chunks(x_ref.shape[0]):
            h = _rms(x_ref[rows, :], g_ref[...]).astype(BF16)
            h_ref[rows, :] = h
            if extra_act is not None:
                z = jnp.dot(h, we_ref[...], preferred_element_type=F32) + be_ref[...]
                e_ref[rows, :] = extra_act(z)[:, :e_ref.shape[1]]
            project(h, rows)

    @pl.when(jnp.logical_not(first))
    def _():
        project(h_ref[...], slice(None))


def _norm_proj(x, g, w, layer, ns, acts, dtypes, *, name, extra=None, tm_pref=1024, tn_pref=512):
    m, d = x.shape
    n_sec = len(acts)
    tm = _pick_tile(m, tm_pref, SUBLANES)
    tn = _pick_tile(ns, tn_pref, LANES)
    tiles = ns // tn
    in_specs = [pl.BlockSpec((tm, d), lambda i, j: (i, 0)),
                pl.BlockSpec((1, d), lambda i, j: (0, 0))]
    in_specs += [pl.BlockSpec((None, d, tn), lambda i, j, s=s: (layer, 0, s * tiles + j)) for s in range(n_sec)]
    args = [x, g.reshape(1, d)] + [w] * n_sec
    out_specs = [pl.BlockSpec((tm, tn), lambda i, j: (i, j)) for _ in range(n_sec)]
    out_shape = [jax.ShapeDtypeStruct((m, ns), dt) for dt in dtypes]
    extra_act = None
    if extra is not None:
        w_e, b_e, n_e, extra_act = extra
        in_specs += [pl.BlockSpec(w_e.shape, lambda i, j: (0, 0)), pl.BlockSpec(b_e.shape, lambda i, j: (0, 0))]
        args += [w_e, b_e]
        out_specs.append(pl.BlockSpec((tm, n_e), lambda i, j: (i, 0)))
        out_shape.append(jax.ShapeDtypeStruct((m, n_e), F32))
    est = 2 * tm * d * 4 + tm * d * 2 + n_sec * 2 * (d * tn * 2 + tm * tn * 4) + 4 * tm * tn * 4
    return pl.pallas_call(
        functools.partial(_norm_proj_kernel, acts=tuple(acts), extra_act=extra_act),
        grid=(m // tm, tiles),
        in_specs=in_specs, out_specs=out_specs, out_shape=out_shape,
        scratch_shapes=[pltpu.VMEM((tm, d), BF16)],
        compiler_params=_params(("parallel", "arbitrary"), est),
        name=name,
    )(*args)


def _matmul_res_kernel(a_ref, w_ref, r_ref, o_ref, *, tn):
    a = a_ref[...]
    for n0 in range(0, o_ref.shape[1], tn):
        o_ref[:, n0:n0 + tn] = r_ref[:, n0:n0 + tn] + jnp.dot(a, w_ref[:, n0:n0 + tn], preferred_element_type=F32)


def _matmul_res(a, w, layer, res, *, name, tm_pref=512, tn_pref=512):
    m, k = a.shape
    n = w.shape[2]
    tm = _pick_tile(m, tm_pref, SUBLANES)
    tn = _pick_tile(n, tn_pref, LANES)
    est = 2 * (tm * k * 2 + k * n * 2 + 2 * tm * n * 4) + 2 * tm * tn * 4
    return pl.pallas_call(
        functools.partial(_matmul_res_kernel, tn=tn),
        grid=(m // tm,),
        in_specs=[pl.BlockSpec((tm, k), lambda i: (i, 0)),
                  pl.BlockSpec((None, k, n), lambda i: (layer, 0, 0)),
                  pl.BlockSpec((tm, n), lambda i: (i, 0))],
        out_specs=pl.BlockSpec((tm, n), lambda i: (i, 0)),
        out_shape=jax.ShapeDtypeStruct((m, n), F32),
        compiler_params=_params(("parallel",), est),
        name=name,
    )(a, w, res)


def _cumsum_kernel(x_ref, o_ref, carry_ref):
    @pl.when(pl.program_id(0) == 0)
    def _():
        carry_ref[...] = jnp.zeros_like(carry_ref)

    x = x_ref[...]
    n = x.shape[1]
    upper = (lax.broadcasted_iota(jnp.int32, (n, n), 0) <= lax.broadcasted_iota(jnp.int32, (n, n), 1)).astype(BF16)
    hi = x.astype(BF16)
    rest = x - hi.astype(F32)
    mid = rest.astype(BF16)
    lo = (rest - mid.astype(F32)).astype(BF16)
    local = (jnp.dot(hi, upper, preferred_element_type=F32) + jnp.dot(mid, upper, preferred_element_type=F32)
             + jnp.dot(lo, upper, preferred_element_type=F32))
    out = local + carry_ref[...]
    o_ref[...] = out
    carry_ref[...] = jnp.broadcast_to(out[:, n - 1:n], out.shape)


def _cumsum_lanes(x, *, name):
    r, l = x.shape
    return pl.pallas_call(
        _cumsum_kernel,
        grid=(l // LANES,),
        in_specs=[pl.BlockSpec((r, LANES), lambda t: (0, t))],
        out_specs=pl.BlockSpec((r, LANES), lambda t: (0, t)),
        out_shape=jax.ShapeDtypeStruct((r, l), F32),
        scratch_shapes=[pltpu.VMEM((r, LANES), F32)],
        compiler_params=_params(("arbitrary",), 8 * r * LANES * 4),
        name=name,
    )(x)


def _cumsum_rows_kernel(x_ref, o_ref, carry_ref):
    @pl.when(pl.program_id(1) == 0)
    def _():
        carry_ref[...] = jnp.zeros_like(carry_ref)

    x = x_ref[0]
    n = x.shape[0]
    lower = (lax.broadcasted_iota(jnp.int32, (n, n), 0) >= lax.broadcasted_iota(jnp.int32, (n, n), 1)).astype(BF16)
    hi = x.astype(BF16)
    rest = x - hi.astype(F32)
    mid = rest.astype(BF16)
    lo = (rest - mid.astype(F32)).astype(BF16)
    local = (jnp.dot(lower, hi, preferred_element_type=F32) + jnp.dot(lower, mid, preferred_element_type=F32)
             + jnp.dot(lower, lo, preferred_element_type=F32))
    out = local + carry_ref[...]
    o_ref[0] = out
    carry_ref[...] = out[n - 1:n, :]


def _cumsum_rows(x, *, name, tile_pref=256):
    b, s_len, w = x.shape
    tile = _pick_tile(s_len, tile_pref, SUBLANES)
    spec = pl.BlockSpec((1, tile, w), lambda bi, ti: (bi, ti, 0))
    return pl.pallas_call(
        _cumsum_rows_kernel,
        grid=(b, s_len // tile),
        in_specs=[spec], out_specs=spec,
        out_shape=jax.ShapeDtypeStruct((b, s_len, w), F32),
        scratch_shapes=[pltpu.VMEM((1, w), F32)],
        compiler_params=_params(("parallel", "arbitrary"), 8 * tile * w * 4 + 4 * tile * tile),
        name=name,
    )(x)


def _fox_attn_kernel(q_ref, k_ref, v_ref, c_ref, o_ref, kaug_ref, vt_ref, ta_ref, tb_ref, pa_ref, pb_ref,
                     acc_ref, m_ref, qt_ref, *, tq, heads, unroll):
    s_len, dh = q_ref.shape[1], q_ref.shape[2]
    nk = s_len // tq
    h = pl.program_id(1)

    @pl.when(h == 0)
    def _():
        head_lane = lax.broadcasted_iota(jnp.int32, c_ref.shape[1:], 1) < heads
        ckp = jnp.where(head_lane, c_ref[0] * (-LOG2E), 0.0)
        hi = ckp.astype(BF16).astype(F32)
        rest = ckp - hi
        mid = rest.astype(BF16).astype(F32)
        lo = (rest - mid).astype(BF16).astype(F32)
        aug = hi + pltpu.roll(mid, heads, axis=1) + pltpu.roll(lo, 2 * heads, axis=1)
        kaug_ref[:, dh:] = aug.astype(BF16)

    kaug_ref[:, :dh] = k_ref[0].astype(BF16)
    for kj in range(nk):
        vt_ref[kj, :dh] = v_ref[0, kj * tq:(kj + 1) * tq, :].T.astype(BF16)
        vt_ref[kj, dh:] = (lax.broadcasted_iota(jnp.int32, (ONES_ROWS, tq), 0) == 0).astype(BF16)

    row = lax.broadcasted_iota(jnp.int32, (LANES, tq), 0)
    pick_t = ((row == h) | (row == heads + h) | (row == 2 * heads + h)).astype(BF16)
    for qi in range(nk):
        qt_ref[qi, :dh] = q_ref[0, qi * tq:(qi + 1) * tq, :].astype(F32).T.astype(BF16)
        qt_ref[qi, dh:] = pick_t
    t_refs, p_refs = (ta_ref, tb_ref), (pa_ref, pb_ref)

    m_ref[...] = jnp.full(m_ref.shape, -jnp.inf, F32)
    acc_ref[...] = jnp.zeros(acc_ref.shape, F32)

    def scores(pair, masked, t_ref):
        qa, ka = pair
        kblk = kaug_ref[pl.ds(pl.multiple_of(ka * tq, tq), tq), :]
        t = jnp.dot(kblk, qt_ref[qa], preferred_element_type=F32)
        if masked:
            causal = lax.broadcasted_iota(jnp.int32, (tq, tq), 0) <= lax.broadcasted_iota(jnp.int32, (tq, tq), 1)
            t = jnp.where(causal, t, -jnp.inf)
        t_ref[...] = t
        return jnp.max(t, axis=0, keepdims=True)

    def softmax_update(pair, t_ref, p_ref, m_blk):
        qa, _ = pair
        m_prev = m_ref[qa]
        m_new = jnp.maximum(m_prev, m_blk)
        alpha = jnp.exp2(m_prev - m_new)
        p_ref[...] = jnp.exp2(t_ref[...] - m_new).astype(BF16)
        m_ref[qa] = m_new
        return alpha

    def accumulate(pair, p_ref, alpha):
        qa, ka = pair
        acc_ref[qa] = alpha * acc_ref[qa] + jnp.dot(vt_ref[ka], p_ref[...], preferred_element_type=F32)

    def pipeline_step(e, pairs, masked, state):
        pair_a, pair_b, pair_c = pairs
        m_blk, alpha = state
        if pair_c is not None:
            accumulate(pair_c, p_refs[e % 2], alpha)
        if pair_b is not None:
            alpha = softmax_update(pair_b, t_refs[(e - 1) % 2], p_refs[(e - 1) % 2], m_blk)
        if pair_a is not None:
            m_blk = scores(pair_a, masked, t_refs[e % 2])
        return m_blk, alpha

    static_pairs = [(qa, qa) for qa in range(nk)] + [(qa, ka) for qa in range(nk) for ka in range(qa)]
    n_steps = len(static_pairs)
    loop_start = nk + nk % 2
    n_loop = max(n_steps - loop_start, 0) // unroll

    def static_step(e, state):
        pairs = [static_pairs[i] if 0 <= i < n_steps else None for i in (e, e - 1, e - 2)]
        return pipeline_step(e, pairs, e < nk, state)

    def advance(pair):
        qa, ka = pair
        wrap = ka + 1 == qa
        return jnp.where(wrap, qa + 1, qa), jnp.where(wrap, 0, ka + 1)

    def body(_, carry):
        state, pairs = carry
        for e in range(unroll):
            state = pipeline_step(e, pairs, False, state)
            pairs = (advance(pairs[0]), pairs[0], pairs[1])
        return state, pairs

    state = (jnp.zeros((1, tq), F32), jnp.zeros((1, tq), F32))
    for e in range(loop_start):
        state = static_step(e, state)
    if n_loop:
        first = tuple((jnp.int32(qa), jnp.int32(ka)) for qa, ka in (static_pairs[loop_start - i] for i in range(3)))
        state, _ = lax.fori_loop(0, n_loop, body, (state, first))
    for e in range(loop_start + unroll * n_loop, n_steps + 2):
        state = static_step(e, state)

    for qi in range(nk):
        out_t = acc_ref[qi, :dh] / acc_ref[qi, dh:dh + 1]
        o_ref[0, qi * tq:(qi + 1) * tq, :] = out_t.T.astype(o_ref.dtype)


def _fox_attn(q, k, v, c, heads, *, name, tq_pref=512, unroll=2):
    b, s_len, d = q.shape
    dh = d // heads
    assert unroll % 2 == 0 and 3 * heads <= LANES
    tq = _pick_tile(s_len, tq_pref, LANES)
    nk = s_len // tq
    qkv_spec = pl.BlockSpec((1, s_len, dh), lambda bi, hi: (bi, 0, hi))
    est = (2 * s_len * dh * (2 + 4 + 4 + 2) + 2 * s_len * LANES * 4 + s_len * (2 * dh + LANES) * 2
           + s_len * (dh + ONES_ROWS + 1) * 4 + s_len * (dh + LANES) * 2 + 3 * tq * tq * 4 + 8 * tq * tq * 4)
    return pl.pallas_call(
        functools.partial(_fox_attn_kernel, tq=tq, heads=heads, unroll=unroll),
        grid=(b, heads),
        in_specs=[qkv_spec, qkv_spec, qkv_spec,
                  pl.BlockSpec((1, s_len, LANES), lambda bi, hi: (bi, 0, 0))],
        out_specs=qkv_spec,
        out_shape=jax.ShapeDtypeStruct((b, s_len, d), BF16),
        scratch_shapes=[pltpu.VMEM((s_len, dh + LANES), BF16), pltpu.VMEM((nk, dh + ONES_ROWS, tq), BF16),
                        pltpu.VMEM((tq, tq), F32), pltpu.VMEM((tq, tq), F32),
                        pltpu.VMEM((tq, tq), BF16), pltpu.VMEM((tq, tq), BF16),
                        pltpu.VMEM((nk, dh + ONES_ROWS, tq), F32), pltpu.VMEM((nk, 1, tq), F32),
                        pltpu.VMEM((nk, dh + LANES, tq), BF16)],
        compiler_params=_params(("parallel", "arbitrary"), est),
        name=name,
    )(q, k, v, c)


def _fox_dec_kernel(q_ref, kn_ref, vn_ref, kc_ref, vc_ref, ckc_ref, ckn_ref, o_ref, m_scr, l_scr, acc_scr,
                    *, heads, scale):
    step = pl.program_id(1)

    @pl.when(step == 0)
    def _():
        m_scr[...] = jnp.full(m_scr.shape, -jnp.inf, F32)
        l_scr[...] = jnp.zeros(l_scr.shape, F32)
        acc_scr[...] = jnp.zeros(acc_scr.shape, F32)

    q = q_ref[0]
    rows, dh = q.shape
    shift = heads.bit_length() - 1

    def absorb(k_flat, v_flat, ck, causal):
        keys = k_flat.shape[0]
        s = lax.dot_general(q, k_flat.astype(BF16), (((1,), (1,)), ((), ())), preferred_element_type=F32) * scale - ck
        r = lax.broadcasted_iota(jnp.int32, (rows, keys), 0)
        j = lax.broadcasted_iota(jnp.int32, (rows, keys), 1)
        visible = ((r ^ j) & (heads - 1)) == 0
        if causal:
            visible = visible & ((j >> shift) <= (r >> shift))
        s = jnp.where(visible, s, -jnp.inf)
        m_prev = m_scr[...]
        m_new = jnp.maximum(m_prev, jnp.max(s, axis=-1, keepdims=True))
        p = jnp.exp(s - m_new)
        alpha = jnp.exp(m_prev - m_new)
        l_scr[...] = alpha * l_scr[...] + jnp.sum(p, axis=-1, keepdims=True)
        acc_scr[...] = alpha * acc_scr[...] + jnp.dot(p.astype(BF16), v_flat.astype(BF16), preferred_element_type=F32)
        m_scr[...] = m_new

    tp = kc_ref.shape[1]
    absorb(kc_ref[0].reshape(tp * heads, dh), vc_ref[0].reshape(tp * heads, dh), ckc_ref[0], False)

    @pl.when(step == pl.num_programs(1) - 1)
    def _():
        absorb(kn_ref[0], vn_ref[0], ckn_ref[0], True)
        o_ref[0] = (acc_scr[...] / l_scr[...]).astype(o_ref.dtype)


def _fox_dec_attn(q, k_new, v_new, k_cache, v_cache, layer, c_cache, c_new, *, name, tp_pref=256):
    b, rows, dh = q.shape
    _, _, p, heads, _ = k_cache.shape
    assert heads & (heads - 1) == 0
    tp = _pick_tile(p, tp_pref, SUBLANES)
    new_spec = pl.BlockSpec((1, rows, dh), lambda bi, pi: (bi, 0, 0))
    cache_spec = pl.BlockSpec((None, 1, tp, heads, dh), lambda bi, pi: (layer, bi, pi, 0, 0))
    est = 2 * 2 * tp * heads * dh * 4 + 2 * tp * heads * dh * 2 + 6 * rows * tp * heads * 4
    return pl.pallas_call(
        functools.partial(_fox_dec_kernel, heads=heads, scale=dh ** -0.5),
        grid=(b, p // tp),
        in_specs=[new_spec, new_spec, new_spec, cache_spec, cache_spec,
                  pl.BlockSpec((1, 1, tp * heads), lambda bi, pi: (bi, 0, pi)),
                  pl.BlockSpec((1, 1, rows), lambda bi, pi: (bi, 0, 0))],
        out_specs=new_spec,
        out_shape=jax.ShapeDtypeStruct((b, rows, dh), BF16),
        scratch_shapes=[pltpu.VMEM((rows, 1), F32), pltpu.VMEM((rows, 1), F32), pltpu.VMEM((rows, dh), F32)],
        compiler_params=_params(("parallel", "arbitrary"), est),
        name=name,
    )(q, k_new, v_new, k_cache, v_cache, c_cache, c_new)


def _lru_kernel(u_ref, gate_ref, cw_ref, cb_ref, wa_ref, ba_ref, wx_ref, bx_ref, lam_ref, h0_ref, buf0_ref,
                y_ref, hl_ref, ul_ref, h_scr, tail_scr):
    @pl.when(pl.program_id(2) == 0)
    def _():
        h_scr[...] = h0_ref[0]
        tail_scr[...] = buf0_ref[0]

    u = u_ref[0]
    tc, bw = u.shape
    taps = cw_ref.shape[0]
    tail = tail_scr[...]
    head_rows = lax.broadcasted_iota(jnp.int32, (SUBLANES, bw), 0)

    def delayed(d):
        if d == 0:
            return u
        ru = pltpu.roll(u, d, axis=0)
        head = jnp.where(head_rows < d, pltpu.roll(tail, d, axis=0), ru[:SUBLANES])
        return head if tc == SUBLANES else jnp.concatenate([head, ru[SUBLANES:]], axis=0)

    conv = delayed(taps - 1) * cw_ref[0:1, :]
    for k in range(1, taps):
        conv = conv + delayed(taps - 1 - k) * cw_ref[k:k + 1, :]
    uc = cb_ref[...] + conv
    tail_scr[...] = u[tc - SUBLANES:, :]
    ul_ref[0] = u[tc - SUBLANES:, :]

    ucb = uc.astype(BF16)
    r = jax.nn.sigmoid(jnp.dot(ucb, wa_ref[0], preferred_element_type=F32) + ba_ref[...])
    ig = jax.nn.sigmoid(jnp.dot(ucb, wx_ref[0], preferred_element_type=F32) + bx_ref[...])
    log_a = (-LRU_C * _softplus(-lam_ref[...])) * r
    a = jnp.exp(log_a)
    th = jnp.tanh(log_a)
    decay = (-2.0 * th) / (1.0 - th)
    b = (decay * lax.rsqrt(jnp.maximum(decay, TINY))) * (ig * uc)

    groups = tc // SUBLANES
    a3 = a.reshape(groups, SUBLANES, bw)
    b3 = b.reshape(groups, SUBLANES, bw)
    sub = lax.broadcasted_iota(jnp.int32, (groups, SUBLANES, bw), 1)
    d = 1
    while d < SUBLANES:
        valid = sub >= d
        b3 = jnp.where(valid, a3 * pltpu.roll(b3, d, axis=1) + b3, b3)
        a3 = jnp.where(valid, a3 * pltpu.roll(a3, d, axis=1), a3)
        d *= 2
    h = h_scr[...]
    hs = []
    for gi in range(groups):
        hg = a3[gi] * h + b3[gi]
        hs.append(hg)
        h = hg[SUBLANES - 1:SUBLANES, :]
    h_scr[...] = h
    hl_ref[0] = hs[-1]
    hs = hs[0] if groups == 1 else jnp.concatenate(hs, axis=0)
    y_ref[0] = (hs * gate_ref[0].astype(F32)).astype(y_ref.dtype)


def _lru_scan(u, gate, conv_w, conv_b, w_ga, b_a, w_gx, b_x, lam, layer, h0, buf0, *, name, tc_pref=512):
    b, t, w = u.shape
    _, nb, bw, _ = w_ga.shape
    taps = conv_w.shape[0]
    tc = _pick_tile(t, tc_pref, SUBLANES)
    seq_spec = pl.BlockSpec((1, tc, bw), lambda bi, ni, ti: (bi, ti, ni))
    row_spec = pl.BlockSpec((1, bw), lambda bi, ni, ti: (0, ni))
    blk_spec = pl.BlockSpec((None, 1, bw, bw), lambda bi, ni, ti: (layer, ni, 0, 0))
    last_spec = pl.BlockSpec((1, SUBLANES, bw), lambda bi, ni, ti: (bi, 0, ni))
    est = 2 * tc * bw * (4 + 2 + 2) + 4 * bw * bw * 2 + 24 * tc * bw * 4
    return pl.pallas_call(
        _lru_kernel,
        grid=(b, nb, t // tc),
        in_specs=[seq_spec, seq_spec,
                  pl.BlockSpec((taps, bw), lambda bi, ni, ti: (0, ni)), row_spec,
                  blk_spec, row_spec, blk_spec, row_spec, row_spec,
                  pl.BlockSpec((1, 1, bw), lambda bi, ni, ti: (bi, 0, ni)), last_spec],
        out_specs=[seq_spec, last_spec, last_spec],
        out_shape=[jax.ShapeDtypeStruct((b, t, w), BF16),
                   jax.ShapeDtypeStruct((b, SUBLANES, w), F32),
                   jax.ShapeDtypeStruct((b, SUBLANES, w), F32)],
        scratch_shapes=[pltpu.VMEM((1, bw), F32), pltpu.VMEM((SUBLANES, bw), F32)],
        compiler_params=_params(("parallel", "parallel", "arbitrary"), est),
        name=name,
    )(u, gate, conv_w, conv_b.reshape(1, w), w_ga, b_a.reshape(1, w), w_gx, b_x.reshape(1, w),
      lam.reshape(1, w), h0, buf0)


def _xattn_kernel(x_ref, g_ref, wq_ref, mk_ref, mv_ref, wo_ref, o_ref, *, heads):
    x = x_ref[0]
    h = _rms(x, g_ref[...]).astype(BF16)
    q = jnp.dot(h, wq_ref[...], preferred_element_type=F32).astype(BF16)
    dh = q.shape[1] // heads
    scale = dh ** -0.5
    outs = []
    for hd in range(heads):
        cols = slice(hd * dh, (hd + 1) * dh)
        s = lax.dot_general(q[:, cols], mk_ref[0, :, cols].astype(BF16), (((1,), (1,)), ((), ())),
                            preferred_element_type=F32) * scale
        p = jnp.exp(s - jnp.max(s, axis=-1, keepdims=True))
        l = jnp.sum(p, axis=-1, keepdims=True)
        o = jnp.dot(p.astype(BF16), mv_ref[0, :, cols].astype(BF16), preferred_element_type=F32)
        outs.append((o / l).astype(BF16))
    o_all = jnp.concatenate(outs, axis=-1)
    o_ref[0] = x + jnp.dot(o_all, wo_ref[...], preferred_element_type=F32)


def _xattn(x, g, wq, mk, mv, wo, layer, heads, *, name, tm_pref=1024):
    b, t, d = x.shape
    nm, mw = mk.shape[1], mk.shape[2]
    tm = _pick_tile(t, tm_pref, SUBLANES)
    x_spec = pl.BlockSpec((1, tm, d), lambda bi, ti: (bi, ti, 0))
    mem_spec = pl.BlockSpec((1, nm, mw), lambda bi, ti: (bi, 0, 0))
    est = 4 * tm * d * 4 + 4 * d * mw * 2 + 4 * nm * mw * 4 + 3 * tm * d * 4
    return pl.pallas_call(
        functools.partial(_xattn_kernel, heads=heads),
        grid=(b, t // tm),
        in_specs=[x_spec, pl.BlockSpec((1, d), lambda bi, ti: (0, 0)),
                  pl.BlockSpec((None, d, mw), lambda bi, ti: (layer, 0, 0)), mem_spec, mem_spec,
                  pl.BlockSpec((None, mw, d), lambda bi, ti: (layer, 0, 0))],
        out_specs=x_spec,
        out_shape=jax.ShapeDtypeStruct((b, t, d), F32),
        compiler_params=_params(("parallel", "parallel"), est),
        name=name,
    )(x, g.reshape(1, d), wq, mk, mv, wo)


def _ffn_kernel(x_ref, g_ref, wg_ref, wu_ref, wo_ref, gf_ref, o_ref, h_ref, *, final_norm):
    j = pl.program_id(1)

    def mix(h):
        gate = jnp.dot(h, wg_ref[...], preferred_element_type=F32)
        up = jnp.dot(h, wu_ref[...], preferred_element_type=F32)
        act = (jax.nn.silu(gate) * up).astype(BF16)
        return jnp.dot(act, wo_ref[...], preferred_element_type=F32)

    @pl.when(j == 0)
    def _():
        for rows in _row_chunks(x_ref.shape[0]):
            x = x_ref[rows, :]
            h = _rms(x, g_ref[...]).astype(BF16)
            h_ref[rows, :] = h
            o_ref[rows, :] = x + mix(h)

    @pl.when(j > 0)
    def _():
        o_ref[...] += mix(h_ref[...])

    if final_norm:
        @pl.when(j == pl.num_programs(1) - 1)
        def _():
            o_ref[...] = _rms(o_ref[...], gf_ref[...])


def _ffn(x, g, w_in, w_out, layer, g_final, *, final_norm, name, tm_pref=1024, tf_pref=512):
    m, d = x.shape
    f = w_out.shape[1]
    tm = _pick_tile(m, tm_pref, SUBLANES)
    tf = _pick_tile(f, tf_pref, LANES)
    nf = f // tf
    x_spec = pl.BlockSpec((tm, d), lambda i, j: (i, 0))
    vec_spec = pl.BlockSpec((1, d), lambda i, j: (0, 0))
    est = 4 * tm * d * 4 + tm * d * 2 + 2 * 3 * d * tf * 2 + 4 * tm * tf * 4
    return pl.pallas_call(
        functools.partial(_ffn_kernel, final_norm=final_norm),
        grid=(m // tm, nf),
        in_specs=[x_spec, vec_spec,
                  pl.BlockSpec((None, d, tf), lambda i, j: (layer, 0, j)),
                  pl.BlockSpec((None, d, tf), lambda i, j: (layer, 0, nf + j)),
                  pl.BlockSpec((None, tf, d), lambda i, j: (layer, j, 0)), vec_spec],
        out_specs=x_spec,
        out_shape=jax.ShapeDtypeStruct((m, d), F32),
        scratch_shapes=[pltpu.VMEM((tm, d), BF16)],
        compiler_params=_params(("parallel", "arbitrary"), est),
        name=name,
    )(x, g.reshape(1, d), w_in, w_in, w_out, g_final.reshape(1, d))


def kernel(x_prompt, x_sample, mem_prompt, cache_fox_k, cache_fox_v, cache_fox_logf, cache_mem_k, cache_mem_v, state_lru_h, state_lru_conv, norm_mix, norm_mem, norm_xattn, norm_ffn, norm_final, fox_w_in, fox_b_f, fox_w_out, lru_w_in, lru_conv_w, lru_conv_b, lru_w_ga, lru_b_a, lru_w_gx, lru_b_x, lru_lambda, lru_w_out, xattn_w_q, xattn_w_kv, xattn_w_o, ffn_w_in, ffn_w_out):
    b, s_len, d = x_prompt.shape
    bd, t_dec, _ = x_sample.shape
    depth = norm_mix.shape[0]
    heads = fox_b_f.shape[1]
    past = cache_fox_k.shape[2]
    n_mem, mem_heads = cache_mem_k.shape[2], cache_mem_k.shape[3]
    mem_w = mem_heads * cache_mem_k.shape[4]
    taps = lru_conv_w.shape[1]
    assert taps - 1 <= SUBLANES <= min(s_len, t_dec) and heads <= LANES

    xp = x_prompt.reshape(b * s_len, d)
    xs = x_sample.reshape(bd * t_dec, d)
    mem = mem_prompt.reshape(b * n_mem, d)
    pk, pv, plf, pmk, pmv, ph, pc = [], [], [], [], [], [], []
    sk, sv, slf, sh, sc = [], [], [], [], []

    fox_in, fox_out = fox_w_in.astype(BF16), fox_w_out.astype(BF16)
    lru_in, lru_out = lru_w_in.astype(BF16), lru_w_out.astype(BF16)
    lru_ga, lru_gx = lru_w_ga.astype(BF16), lru_w_gx.astype(BF16)
    xq, xkv, xo = xattn_w_q.astype(BF16), xattn_w_kv.astype(BF16), xattn_w_o.astype(BF16)
    ffn_in, ffn_out = ffn_w_in.astype(BF16), ffn_w_out.astype(BF16)

    for i in range(depth):
        j = i // 2
        if i % 2 == 0:
            w_f = jnp.pad(fox_in[j, :, 3 * d:], ((0, 0), (0, LANES - heads)))
            b_f = jnp.pad(fox_b_f[j], (0, LANES - heads)).reshape(1, LANES)
            q_scale = (d // heads) ** -0.5 * LOG2E
            proj = functools.partial(_norm_proj, g=norm_mix[i], w=fox_in, layer=j, ns=d, dtypes=[BF16, F32, F32],
                                     extra=(w_f, b_f, LANES, _log_sigmoid))
            proj_p = functools.partial(proj, acts=[lambda q: q * q_scale, _identity, _identity])
            proj_s = functools.partial(proj, acts=[_identity] * 3)

            qp, kp, vp, lfp = proj_p(xp, name=f"fox_proj_p{i}")
            c_p = _cumsum_rows(lfp.reshape(b, s_len, LANES), name=f"fox_cumsum_p{i}")
            attn = _fox_attn(qp.reshape(b, s_len, d), kp.reshape(b, s_len, d), vp.reshape(b, s_len, d), c_p, heads,
                             name=f"fox_attn_p{i}")
            xp = _matmul_res(attn.reshape(b * s_len, d), fox_out, j, xp, name=f"fox_out_p{i}")
            pk.append(kp.reshape(b, s_len, heads, d // heads))
            pv.append(vp.reshape(b, s_len, heads, d // heads))
            plf.append(lfp[:, :heads].reshape(b, s_len, heads))

            qs, ks, vs, lfs = proj_s(xs, name=f"fox_proj_s{i}")
            lfs = lfs[:, :heads]
            lf_all = jnp.concatenate([cache_fox_logf[j], lfs.reshape(bd, t_dec, heads)], axis=1)
            total = past + t_dec
            padded = -(-total // LANES) * LANES
            lf_t = jnp.pad(lf_all.transpose(0, 2, 1).reshape(bd * heads, total), ((0, 0), (0, padded - total)))
            c_s = _cumsum_lanes(lf_t, name=f"fox_cumsum_s{i}")
            c_s = c_s.reshape(bd, heads, padded).transpose(0, 2, 1)
            rows = t_dec * heads
            attn = _fox_dec_attn(qs.reshape(bd, rows, d // heads), ks.reshape(bd, rows, d // heads),
                                 vs.reshape(bd, rows, d // heads), cache_fox_k, cache_fox_v, j,
                                 c_s[:, :past].reshape(bd, 1, past * heads), c_s[:, past:total].reshape(bd, 1, rows),
                                 name=f"fox_attn_s{i}")
            xs = _matmul_res(attn.reshape(bd * t_dec, d), fox_out, j, xs, name=f"fox_out_s{i}")
            sk.append(ks.reshape(bd, t_dec, heads, d // heads))
            sv.append(vs.reshape(bd, t_dec, heads, d // heads))
            slf.append(lfs.reshape(bd, t_dec, heads))
        else:
            width = lru_in.shape[2] // 2
            proj = functools.partial(_norm_proj, g=norm_mix[i], w=lru_in, layer=j, ns=width,
                                     acts=[jax.nn.gelu, _identity], dtypes=[BF16, F32])
            scan = functools.partial(_lru_scan, conv_w=lru_conv_w[j], conv_b=lru_conv_b[j], w_ga=lru_ga, b_a=lru_b_a[j],
                                     w_gx=lru_gx, b_x=lru_b_x[j], lam=lru_lambda[j], layer=j)

            gate, u = proj(xp, name=f"lru_proj_p{i}")
            y, h8, u8 = scan(u.reshape(b, s_len, width), gate.reshape(b, s_len, width),
                             h0=jnp.zeros((b, 1, width), F32), buf0=jnp.zeros((b, SUBLANES, width), F32),
                             name=f"lru_scan_p{i}")
            xp = _matmul_res(y.reshape(b * s_len, width), lru_out, j, xp, name=f"lru_out_p{i}")
            ph.append(h8[:, SUBLANES - 1])
            pc.append(u8[:, SUBLANES - (taps - 1):])

            gate, u = proj(xs, name=f"lru_proj_s{i}")
            buf0 = jnp.pad(state_lru_conv[j], ((0, 0), (SUBLANES - (taps - 1), 0), (0, 0)))
            y, h8, u8 = scan(u.reshape(bd, t_dec, width), gate.reshape(bd, t_dec, width),
                             h0=state_lru_h[j].reshape(bd, 1, width), buf0=buf0, name=f"lru_scan_s{i}")
            xs = _matmul_res(y.reshape(bd * t_dec, width), lru_out, j, xs, name=f"lru_out_s{i}")
            sh.append(h8[:, SUBLANES - 1])
            sc.append(u8[:, SUBLANES - (taps - 1):])

        mk, mv = _norm_proj(mem, norm_mem[i], xkv, i, mem_w, [_identity] * 2, [F32, F32], name=f"mem_kv{i}")
        pmk.append(mk.reshape(b, n_mem, mem_heads, mem_w // mem_heads))
        pmv.append(mv.reshape(b, n_mem, mem_heads, mem_w // mem_heads))
        xp = _xattn(xp.reshape(b, s_len, d), norm_xattn[i], xq, mk.reshape(b, n_mem, mem_w), mv.reshape(b, n_mem, mem_w),
                    xo, i, mem_heads, name=f"xattn_p{i}").reshape(b * s_len, d)
        xs = _xattn(xs.reshape(bd, t_dec, d), norm_xattn[i], xq, cache_mem_k[i].reshape(bd, n_mem, mem_w),
                    cache_mem_v[i].reshape(bd, n_mem, mem_w), xo, i, mem_heads, name=f"xattn_s{i}").reshape(bd * t_dec, d)

        last = i == depth - 1
        xp = _ffn(xp, norm_ffn[i], ffn_in, ffn_out, i, norm_final, final_norm=last, name=f"ffn_p{i}")
        xs = _ffn(xs, norm_ffn[i], ffn_in, ffn_out, i, norm_final, final_norm=last, name=f"ffn_s{i}")

    return (xp.reshape(b, s_len, d), xs.reshape(bd, t_dec, d),
            jnp.stack(pk), jnp.stack(pv), jnp.stack(plf), jnp.stack(pmk), jnp.stack(pmv),
            jnp.stack(ph), jnp.stack(pc),
            jnp.stack(sk), jnp.stack(sv), jnp.stack(slf), jnp.stack(sh), jnp.stack(sc))
```

```python
import functools

import jax
import jax.numpy as jnp
from jax import lax
from jax.experimental import pallas as pl
from jax.experimental.pallas import tpu as pltpu

F32 = jnp.float32
BF16 = jnp.bfloat16

EPS = 1e-6
LOG2E = 1.4426950408889634
TINY = 1e-30
LRU_C = 8.0
LANES = 128
SUBLANES = 8
ONES_ROWS = 16
NORM_CHUNKS = 4
V7X_VMEM_BUDGET = 56 * 1024 * 1024


def _pick_tile(n, pref, align):
    if n <= pref:
        return n
    t = (pref // align) * align
    while t >= align:
        if n % t == 0:
            return t
        t -= align
    raise ValueError(f"no {align}-aligned tile of {n} below {pref}")


def _params(semantics, vmem_estimate):
    limit = int(min(V7X_VMEM_BUDGET, max(2 * vmem_estimate, 16 * 1024 * 1024)))
    return pltpu.CompilerParams(dimension_semantics=semantics, vmem_limit_bytes=limit)


def _rms(xf, g):
    return xf * lax.rsqrt(jnp.mean(xf * xf, axis=-1, keepdims=True) + EPS) * g


def _identity(x):
    return x


def _row_chunks(rows):
    step = rows // NORM_CHUNKS if rows % (NORM_CHUNKS * 2 * SUBLANES) == 0 else rows
    return [slice(r0, r0 + step) for r0 in range(0, rows, step)]


def _softplus(x):
    return jnp.maximum(x, 0.0) + jnp.log1p(jnp.exp(-jnp.abs(x)))


def _log_sigmoid(x):
    return -_softplus(-x)


def _norm_proj_kernel(*refs, acts, extra_act):
    n_sec = len(acts)
    x_ref, g_ref = refs[0], refs[1]
    w_refs = refs[2:2 + n_sec]
    pos = 2 + n_sec
    if extra_act is not None:
        we_ref, be_ref = refs[pos], refs[pos + 1]
        pos += 2
    o_refs = refs[pos:pos + n_sec]
    pos += n_sec
    if extra_act is not None:
        e_ref = refs[pos]
        pos += 1
    h_ref = refs[pos]

    def project(h, rows):
        for w_ref, o_ref, act in zip(w_refs, o_refs, acts):
            o_ref[rows, :] = act(jnp.dot(h, w_ref[...], preferred_element_type=F32)).astype(o_ref.dtype)

    first = pl.program_id(1) == 0

    @pl.when(first)
    def _():
        for rows in _row_chunks(x_ref.shape[0]):
            h = _rms(x_ref[rows, :], g_ref[...]).astype(BF16)
            h_ref[rows, :] = h
            if extra_act is not None:
                z = jnp.dot(h, we_ref[...], preferred_element_type=F32) + be_ref[...]
                e_ref[rows, :] = extra_act(z)[:, :e_ref.shape[1]]
            project(h, rows)

    @pl.when(jnp.logical_not(first))
    def _():
        project(h_ref[...], slice(None))


def _norm_proj(x, g, w, layer, ns, acts, dtypes, *, name, extra=None, tm_pref=1024, tn_pref=512):
    m, d = x.shape
    n_sec = len(acts)
    tm = _pick_tile(m, tm_pref, SUBLANES)
    tn = _pick_tile(ns, tn_pref, LANES)
    tiles = ns // tn
    in_specs = [pl.BlockSpec((tm, d), lambda i, j: (i, 0)),
                pl.BlockSpec((1, d), lambda i, j: (0, 0))]
    in_specs += [pl.BlockSpec((None, d, tn), lambda i, j, s=s: (layer, 0, s * tiles + j)) for s in range(n_sec)]
    args = [x, g.reshape(1, d)] + [w] * n_sec
    out_specs = [pl.BlockSpec((tm, tn), lambda i, j: (i, j)) for _ in range(n_sec)]
    out_shape = [jax.ShapeDtypeStruct((m, ns), dt) for dt in dtypes]
    extra_act = None
    if extra is not None:
        w_e, b_e, n_e, extra_act = extra
        in_specs += [pl.BlockSpec(w_e.shape, lambda i, j: (0, 0)), pl.BlockSpec(b_e.shape, lambda i, j: (0, 0))]
        args += [w_e, b_e]
        out_specs.append(pl.BlockSpec((tm, n_e), lambda i, j: (i, 0)))
        out_shape.append(jax.ShapeDtypeStruct((m, n_e), F32))
    est = 2 * tm * d * 4 + tm * d * 2 + n_sec * 2 * (d * tn * 2 + tm * tn * 4) + 4 * tm * tn * 4
    return pl.pallas_call(
        functools.partial(_norm_proj_kernel, acts=tuple(acts), extra_act=extra_act),
        grid=(m // tm, tiles),
        in_specs=in_specs, out_specs=out_specs, out_shape=out_shape,
        scratch_shapes=[pltpu.VMEM((tm, d), BF16)],
        compiler_params=_params(("parallel", "arbitrary"), est),
        name=name,
    )(*args)


def _matmul_res_kernel(a_ref, w_ref, r_ref, o_ref, *, tn):
    a = a_ref[...]
    for n0 in range(0, o_ref.shape[1], tn):
        o_ref[:, n0:n0 + tn] = r_ref[:, n0:n0 + tn] + jnp.dot(a, w_ref[:, n0:n0 + tn], preferred_element_type=F32)


def _matmul_res(a, w, layer, res, *, name, tm_pref=512, tn_pref=512):
    m, k = a.shape
    n = w.shape[2]
    tm = _pick_tile(m, tm_pref, SUBLANES)
    tn = _pick_tile(n, tn_pref, LANES)
    est = 2 * (tm * k * 2 + k * n * 2 + 2 * tm * n * 4) + 2 * tm * tn * 4
    return pl.pallas_call(
        functools.partial(_matmul_res_kernel, tn=tn),
        grid=(m // tm,),
        in_specs=[pl.BlockSpec((tm, k), lambda i: (i, 0)),
                  pl.BlockSpec((None, k, n), lambda i: (layer, 0, 0)),
                  pl.BlockSpec((tm, n), lambda i: (i, 0))],
        out_specs=pl.BlockSpec((tm, n), lambda i: (i, 0)),
        out_shape=jax.ShapeDtypeStruct((m, n), F32),
        compiler_params=_params(("parallel",), est),
        name=name,
    )(a, w, res)


def _cumsum_kernel(x_ref, o_ref, carry_ref):
    @pl.when(pl.program_id(0) == 0)
    def _():
        carry_ref[...] = jnp.zeros_like(carry_ref)

    x = x_ref[...]
    n = x.shape[1]
    upper = (lax.broadcasted_iota(jnp.int32, (n, n), 0) <= lax.broadcasted_iota(jnp.int32, (n, n), 1)).astype(BF16)
    hi = x.astype(BF16)
    rest = x - hi.astype(F32)
    mid = rest.astype(BF16)
    lo = (rest - mid.astype(F32)).astype(BF16)
    local = (jnp.dot(hi, upper, preferred_element_type=F32) + jnp.dot(mid, upper, preferred_element_type=F32)
             + jnp.dot(lo, upper, preferred_element_type=F32))
    out = local + carry_ref[...]
    o_ref[...] = out
    carry_ref[...] = jnp.broadcast_to(out[:, n - 1:n], out.shape)


def _cumsum_lanes(x, *, name):
    r, l = x.shape
    return pl.pallas_call(
        _cumsum_kernel,
        grid=(l // LANES,),
        in_specs=[pl.BlockSpec((r, LANES), lambda t: (0, t))],
        out_specs=pl.BlockSpec((r, LANES), lambda t: (0, t)),
        out_shape=jax.ShapeDtypeStruct((r, l), F32),
        scratch_shapes=[pltpu.VMEM((r, LANES), F32)],
        compiler_params=_params(("arbitrary",), 8 * r * LANES * 4),
        name=name,
    )(x)


def _cumsum_rows_kernel(x_ref, o_ref, carry_ref):
    @pl.when(pl.program_id(1) == 0)
    def _():
        carry_ref[...] = jnp.zeros_like(carry_ref)

    x = x_ref[0]
    n = x.shape[0]
    lower = (lax.broadcasted_iota(jnp.int32, (n, n), 0) >= lax.broadcasted_iota(jnp.int32, (n, n), 1)).astype(BF16)
    hi = x.astype(BF16)
    rest = x - hi.astype(F32)
    mid = rest.astype(BF16)
    lo = (rest - mid.astype(F32)).astype(BF16)
    local = (jnp.dot(lower, hi, preferred_element_type=F32) + jnp.dot(lower, mid, preferred_element_type=F32)
             + jnp.dot(lower, lo, preferred_element_type=F32))
    out = local + carry_ref[...]
    o_ref[0] = out
    carry_ref[...] = out[n - 1:n, :]


def _cumsum_rows(x, *, name, tile_pref=256):
    b, s_len, w = x.shape
    tile = _pick_tile(s_len, tile_pref, SUBLANES)
    spec = pl.BlockSpec((1, tile, w), lambda bi, ti: (bi, ti, 0))
    return pl.pallas_call(
        _cumsum_rows_kernel,
        grid=(b, s_len // tile),
        in_specs=[spec], out_specs=spec,
        out_shape=jax.ShapeDtypeStruct((b, s_len, w), F32),
        scratch_shapes=[pltpu.VMEM((1, w), F32)],
        compiler_params=_params(("parallel", "arbitrary"), 8 * tile * w * 4 + 4 * tile * tile),
        name=name,
    )(x)


def _fox_attn_kernel(q_ref, k_ref, v_ref, c_ref, o_ref, kaug_ref, vt_ref, ta_ref, tb_ref, pa_ref, pb_ref,
                     acc_ref, m_ref, qt_ref, *, tq, heads):
    s_len, dh = q_ref.shape[1], q_ref.shape[2]
    nk = s_len // tq
    h = pl.program_id(1)

    @pl.when(h == 0)
    def _():
        head_lane = lax.broadcasted_iota(jnp.int32, c_ref.shape[1:], 1) < heads
        ckp = jnp.where(head_lane, c_ref[0] * (-LOG2E), 0.0)
        hi = ckp.astype(BF16).astype(F32)
        rest = ckp - hi
        mid = rest.astype(BF16).astype(F32)
        lo = (rest - mid).astype(BF16).astype(F32)
        aug = hi + pltpu.roll(mid, heads, axis=1) + pltpu.roll(lo, 2 * heads, axis=1)
        kaug_ref[:, dh:] = aug.astype(BF16)

    kaug_ref[:, :dh] = k_ref[0].astype(BF16)
    for kj in range(nk):
        vt_ref[kj, :dh] = v_ref[0, kj * tq:(kj + 1) * tq, :].T.astype(BF16)
        vt_ref[kj, dh:] = (lax.broadcasted_iota(jnp.int32, (ONES_ROWS, tq), 0) == 0).astype(BF16)

    row = lax.broadcasted_iota(jnp.int32, (LANES, tq), 0)
    pick_t = ((row == h) | (row == heads + h) | (row == 2 * heads + h)).astype(BF16)
    for qi in range(nk):
        qt_ref[qi, :dh] = q_ref[0, qi * tq:(qi + 1) * tq, :].astype(F32).T.astype(BF16)
        qt_ref[qi, dh:] = pick_t
    t_refs, p_refs = (ta_ref, tb_ref), (pa_ref, pb_ref)

    m_ref[...] = jnp.full(m_ref.shape, -jnp.inf, F32)
    acc_ref[...] = jnp.zeros(acc_ref.shape, F32)

    def scores(pair, masked, t_ref):
        qa, ka = pair
        t = jnp.dot(kaug_ref[ka * tq:(ka + 1) * tq, :], qt_ref[qa], preferred_element_type=F32)
        if masked:
            causal = lax.broadcasted_iota(jnp.int32, (tq, tq), 0) <= lax.broadcasted_iota(jnp.int32, (tq, tq), 1)
            t = jnp.where(causal, t, -jnp.inf)
        t_ref[...] = t
        return jnp.max(t, axis=0, keepdims=True)

    def softmax_update(pair, t_ref, p_ref, m_blk):
        qa, _ = pair
        m_prev = m_ref[qa]
        m_new = jnp.maximum(m_prev, m_blk)
        alpha = jnp.exp2(m_prev - m_new)
        p_ref[...] = jnp.exp2(t_ref[...] - m_new).astype(BF16)
        m_ref[qa] = m_new
        return alpha

    def accumulate(pair, p_ref, alpha):
        qa, ka = pair
        acc_ref[qa] = alpha * acc_ref[qa] + jnp.dot(vt_ref[ka], p_ref[...], preferred_element_type=F32)

    pairs = [(qa, qa) for qa in range(nk)] + [(qa, ka) for qa in range(nk) for ka in range(qa)]
    m_blk = alpha = None
    for e in range(len(pairs) + 2):
        if e >= 2:
            accumulate(pairs[e - 2], p_refs[e % 2], alpha)
        if 1 <= e <= len(pairs):
            alpha = softmax_update(pairs[e - 1], t_refs[(e - 1) % 2], p_refs[(e - 1) % 2], m_blk)
        if e < len(pairs):
            m_blk = scores(pairs[e], e < nk, t_refs[e % 2])

    for qi in range(nk):
        out_t = acc_ref[qi, :dh] / acc_ref[qi, dh:dh + 1]
        o_ref[0, qi * tq:(qi + 1) * tq, :] = out_t.T.astype(o_ref.dtype)


def _fox_attn(q, k, v, c, heads, *, name, tq_pref=512):
    b, s_len, d = q.shape
    dh = d // heads
    assert 3 * heads <= LANES
    tq = _pick_tile(s_len, tq_pref, LANES)
    nk = s_len // tq
    qkv_spec = pl.BlockSpec((1, s_len, dh), lambda bi, hi: (bi, 0, hi))
    est = (2 * s_len * dh * (2 + 4 + 4 + 2) + 2 * s_len * LANES * 4 + s_len * (2 * dh + LANES) * 2
           + s_len * (dh + ONES_ROWS + 1) * 4 + s_len * (dh + LANES) * 2 + 3 * tq * tq * 4 + 8 * tq * tq * 4)
    return pl.pallas_call(
        functools.partial(_fox_attn_kernel, tq=tq, heads=heads),
        grid=(b, heads),
        in_specs=[qkv_spec, qkv_spec, qkv_spec,
                  pl.BlockSpec((1, s_len, LANES), lambda bi, hi: (bi, 0, 0))],
        out_specs=qkv_spec,
        out_shape=jax.ShapeDtypeStruct((b, s_len, d), BF16),
        scratch_shapes=[pltpu.VMEM((s_len, dh + LANES), BF16), pltpu.VMEM((nk, dh + ONES_ROWS, tq), BF16),
                        pltpu.VMEM((tq, tq), F32), pltpu.VMEM((tq, tq), F32),
                        pltpu.VMEM((tq, tq), BF16), pltpu.VMEM((tq, tq), BF16),
                        pltpu.VMEM((nk, dh + ONES_ROWS, tq), F32), pltpu.VMEM((nk, 1, tq), F32),
                        pltpu.VMEM((nk, dh + LANES, tq), BF16)],
        compiler_params=_params(("parallel", "arbitrary"), est),
        name=name,
    )(q, k, v, c)


def _fox_dec_kernel(q_ref, kn_ref, vn_ref, kc_ref, vc_ref, ckc_ref, ckn_ref, o_ref, m_scr, l_scr, acc_scr,
                    *, heads, scale):
    step = pl.program_id(1)

    @pl.when(step == 0)
    def _():
        m_scr[...] = jnp.full(m_scr.shape, -jnp.inf, F32)
        l_scr[...] = jnp.zeros(l_scr.shape, F32)
        acc_scr[...] = jnp.zeros(acc_scr.shape, F32)

    q = q_ref[0]
    rows, dh = q.shape
    shift = heads.bit_length() - 1

    def absorb(k_flat, v_flat, ck, causal):
        keys = k_flat.shape[0]
        s = lax.dot_general(q, k_flat.astype(BF16), (((1,), (1,)), ((), ())), preferred_element_type=F32) * scale - ck
        r = lax.broadcasted_iota(jnp.int32, (rows, keys), 0)
        j = lax.broadcasted_iota(jnp.int32, (rows, keys), 1)
        visible = ((r ^ j) & (heads - 1)) == 0
        if causal:
            visible = visible & ((j >> shift) <= (r >> shift))
        s = jnp.where(visible, s, -jnp.inf)
        m_prev = m_scr[...]
        m_new = jnp.maximum(m_prev, jnp.max(s, axis=-1, keepdims=True))
        p = jnp.exp(s - m_new)
        alpha = jnp.exp(m_prev - m_new)
        l_scr[...] = alpha * l_scr[...] + jnp.sum(p, axis=-1, keepdims=True)
        acc_scr[...] = alpha * acc_scr[...] + jnp.dot(p.astype(BF16), v_flat.astype(BF16), preferred_element_type=F32)
        m_scr[...] = m_new

    tp = kc_ref.shape[1]
    absorb(kc_ref[0].reshape(tp * heads, dh), vc_ref[0].reshape(tp * heads, dh), ckc_ref[0], False)

    @pl.when(step == pl.num_programs(1) - 1)
    def _():
        absorb(kn_ref[0], vn_ref[0], ckn_ref[0], True)
        o_ref[0] = (acc_scr[...] / l_scr[...]).astype(o_ref.dtype)


def _fox_dec_attn(q, k_new, v_new, k_cache, v_cache, layer, c_cache, c_new, *, name, tp_pref=256):
    b, rows, dh = q.shape
    _, _, p, heads, _ = k_cache.shape
    assert heads & (heads - 1) == 0
    tp = _pick_tile(p, tp_pref, SUBLANES)
    new_spec = pl.BlockSpec((1, rows, dh), lambda bi, pi: (bi, 0, 0))
    cache_spec = pl.BlockSpec((None, 1, tp, heads, dh), lambda bi, pi: (layer, bi, pi, 0, 0))
    est = 2 * 2 * tp * heads * dh * 4 + 2 * tp * heads * dh * 2 + 6 * rows * tp * heads * 4
    return pl.pallas_call(
        functools.partial(_fox_dec_kernel, heads=heads, scale=dh ** -0.5),
        grid=(b, p // tp),
        in_specs=[new_spec, new_spec, new_spec, cache_spec, cache_spec,
                  pl.BlockSpec((1, 1, tp * heads), lambda bi, pi: (bi, 0, pi)),
                  pl.BlockSpec((1, 1, rows), lambda bi, pi: (bi, 0, 0))],
        out_specs=new_spec,
        out_shape=jax.ShapeDtypeStruct((b, rows, dh), BF16),
        scratch_shapes=[pltpu.VMEM((rows, 1), F32), pltpu.VMEM((rows, 1), F32), pltpu.VMEM((rows, dh), F32)],
        compiler_params=_params(("parallel", "arbitrary"), est),
        name=name,
    )(q, k_new, v_new, k_cache, v_cache, c_cache, c_new)


def _lru_kernel(u_ref, gate_ref, cw_ref, cb_ref, wa_ref, ba_ref, wx_ref, bx_ref, lam_ref, h0_ref, buf0_ref,
                y_ref, hl_ref, ul_ref, h_scr, tail_scr):
    @pl.when(pl.program_id(2) == 0)
    def _():
        h_scr[...] = h0_ref[0]
        tail_scr[...] = buf0_ref[0]

    u = u_ref[0]
    tc, bw = u.shape
    taps = cw_ref.shape[0]
    tail = tail_scr[...]
    head_rows = lax.broadcasted_iota(jnp.int32, (SUBLANES, bw), 0)

    def delayed(d):
        if d == 0:
            return u
        ru = pltpu.roll(u, d, axis=0)
        head = jnp.where(head_rows < d, pltpu.roll(tail, d, axis=0), ru[:SUBLANES])
        return head if tc == SUBLANES else jnp.concatenate([head, ru[SUBLANES:]], axis=0)

    conv = delayed(taps - 1) * cw_ref[0:1, :]
    for k in range(1, taps):
        conv = conv + delayed(taps - 1 - k) * cw_ref[k:k + 1, :]
    uc = cb_ref[...] + conv
    tail_scr[...] = u[tc - SUBLANES:, :]
    ul_ref[0] = u[tc - SUBLANES:, :]

    ucb = uc.astype(BF16)
    r = jax.nn.sigmoid(jnp.dot(ucb, wa_ref[0], preferred_element_type=F32) + ba_ref[...])
    ig = jax.nn.sigmoid(jnp.dot(ucb, wx_ref[0], preferred_element_type=F32) + bx_ref[...])
    log_a = (-LRU_C * _softplus(-lam_ref[...])) * r
    a = jnp.exp(log_a)
    th = jnp.tanh(log_a)
    decay = (-2.0 * th) / (1.0 - th)
    b = (decay * lax.rsqrt(jnp.maximum(decay, TINY))) * (ig * uc)

    groups = tc // SUBLANES
    a3 = a.reshape(groups, SUBLANES, bw)
    b3 = b.reshape(groups, SUBLANES, bw)
    sub = lax.broadcasted_iota(jnp.int32, (groups, SUBLANES, bw), 1)
    d = 1
    while d < SUBLANES:
        valid = sub >= d
        b3 = jnp.where(valid, a3 * pltpu.roll(b3, d, axis=1) + b3, b3)
        a3 = jnp.where(valid, a3 * pltpu.roll(a3, d, axis=1), a3)
        d *= 2
    h = h_scr[...]
    hs = []
    for gi in range(groups):
        hg = a3[gi] * h + b3[gi]
        hs.append(hg)
        h = hg[SUBLANES - 1:SUBLANES, :]
    h_scr[...] = h
    hl_ref[0] = hs[-1]
    hs = hs[0] if groups == 1 else jnp.concatenate(hs, axis=0)
    y_ref[0] = (hs * gate_ref[0].astype(F32)).astype(y_ref.dtype)


def _lru_scan(u, gate, conv_w, conv_b, w_ga, b_a, w_gx, b_x, lam, layer, h0, buf0, *, name, tc_pref=512):
    b, t, w = u.shape
    _, nb, bw, _ = w_ga.shape
    taps = conv_w.shape[0]
    tc = _pick_tile(t, tc_pref, SUBLANES)
    seq_spec = pl.BlockSpec((1, tc, bw), lambda bi, ni, ti: (bi, ti, ni))
    row_spec = pl.BlockSpec((1, bw), lambda bi, ni, ti: (0, ni))
    blk_spec = pl.BlockSpec((None, 1, bw, bw), lambda bi, ni, ti: (layer, ni, 0, 0))
    last_spec = pl.BlockSpec((1, SUBLANES, bw), lambda bi, ni, ti: (bi, 0, ni))
    est = 2 * tc * bw * (4 + 2 + 2) + 4 * bw * bw * 2 + 24 * tc * bw * 4
    return pl.pallas_call(
        _lru_kernel,
        grid=(b, nb, t // tc),
        in_specs=[seq_spec, seq_spec,
                  pl.BlockSpec((taps, bw), lambda bi, ni, ti: (0, ni)), row_spec,
                  blk_spec, row_spec, blk_spec, row_spec, row_spec,
                  pl.BlockSpec((1, 1, bw), lambda bi, ni, ti: (bi, 0, ni)), last_spec],
        out_specs=[seq_spec, last_spec, last_spec],
        out_shape=[jax.ShapeDtypeStruct((b, t, w), BF16),
                   jax.ShapeDtypeStruct((b, SUBLANES, w), F32),
                   jax.ShapeDtypeStruct((b, SUBLANES, w), F32)],
        scratch_shapes=[pltpu.VMEM((1, bw), F32), pltpu.VMEM((SUBLANES, bw), F32)],
        compiler_params=_params(("parallel", "parallel", "arbitrary"), est),
        name=name,
    )(u, gate, conv_w, conv_b.reshape(1, w), w_ga, b_a.reshape(1, w), w_gx, b_x.reshape(1, w),
      lam.reshape(1, w), h0, buf0)


def _xattn_kernel(x_ref, g_ref, wq_ref, mk_ref, mv_ref, wo_ref, o_ref, *, heads):
    x = x_ref[0]
    h = _rms(x, g_ref[...]).astype(BF16)
    q = jnp.dot(h, wq_ref[...], preferred_element_type=F32).astype(BF16)
    dh = q.shape[1] // heads
    scale = dh ** -0.5
    outs = []
    for hd in range(heads):
        cols = slice(hd * dh, (hd + 1) * dh)
        s = lax.dot_general(q[:, cols], mk_ref[0, :, cols].astype(BF16), (((1,), (1,)), ((), ())),
                            preferred_element_type=F32) * scale
        p = jnp.exp(s - jnp.max(s, axis=-1, keepdims=True))
        l = jnp.sum(p, axis=-1, keepdims=True)
        o = jnp.dot(p.astype(BF16), mv_ref[0, :, cols].astype(BF16), preferred_element_type=F32)
        outs.append((o / l).astype(BF16))
    o_all = jnp.concatenate(outs, axis=-1)
    o_ref[0] = x + jnp.dot(o_all, wo_ref[...], preferred_element_type=F32)


def _xattn(x, g, wq, mk, mv, wo, layer, heads, *, name, tm_pref=1024):
    b, t, d = x.shape
    nm, mw = mk.shape[1], mk.shape[2]
    tm = _pick_tile(t, tm_pref, SUBLANES)
    x_spec = pl.BlockSpec((1, tm, d), lambda bi, ti: (bi, ti, 0))
    mem_spec = pl.BlockSpec((1, nm, mw), lambda bi, ti: (bi, 0, 0))
    est = 4 * tm * d * 4 + 4 * d * mw * 2 + 4 * nm * mw * 4 + 3 * tm * d * 4
    return pl.pallas_call(
        functools.partial(_xattn_kernel, heads=heads),
        grid=(b, t // tm),
        in_specs=[x_spec, pl.BlockSpec((1, d), lambda bi, ti: (0, 0)),
                  pl.BlockSpec((None, d, mw), lambda bi, ti: (layer, 0, 0)), mem_spec, mem_spec,
                  pl.BlockSpec((None, mw, d), lambda bi, ti: (layer, 0, 0))],
        out_specs=x_spec,
        out_shape=jax.ShapeDtypeStruct((b, t, d), F32),
        compiler_params=_params(("parallel", "parallel"), est),
        name=name,
    )(x, g.reshape(1, d), wq, mk, mv, wo)


def _ffn_kernel(x_ref, g_ref, wg_ref, wu_ref, wo_ref, gf_ref, o_ref, h_ref, *, final_norm):
    j = pl.program_id(1)

    def mix(h):
        gate = jnp.dot(h, wg_ref[...], preferred_element_type=F32)
        up = jnp.dot(h, wu_ref[...], preferred_element_type=F32)
        act = (jax.nn.silu(gate) * up).astype(BF16)
        return jnp.dot(act, wo_ref[...], preferred_element_type=F32)

    @pl.when(j == 0)
    def _():
        for rows in _row_chunks(x_ref.shape[0]):
            x = x_ref[rows, :]
            h = _rms(x, g_ref[...]).astype(BF16)
            h_ref[rows, :] = h
            o_ref[rows, :] = x + mix(h)

    @pl.when(j > 0)
    def _():
        o_ref[...] += mix(h_ref[...])

    if final_norm:
        @pl.when(j == pl.num_programs(1) - 1)
        def _():
            o_ref[...] = _rms(o_ref[...], gf_ref[...])


def _ffn(x, g, w_in, w_out, layer, g_final, *, final_norm, name, tm_pref=1024, tf_pref=512):
    m, d = x.shape
    f = w_out.shape[1]
    tm = _pick_tile(m, tm_pref, SUBLANES)
    tf = _pick_tile(f, tf_pref, LANES)
    nf = f // tf
    x_spec = pl.BlockSpec((tm, d), lambda i, j: (i, 0))
    vec_spec = pl.BlockSpec((1, d), lambda i, j: (0, 0))
    est = 4 * tm * d * 4 + tm * d * 2 + 2 * 3 * d * tf * 2 + 4 * tm * tf * 4
    return pl.pallas_call(
        functools.partial(_ffn_kernel, final_norm=final_norm),
        grid=(m // tm, nf),
        in_specs=[x_spec, vec_spec,
                  pl.BlockSpec((None, d, tf), lambda i, j: (layer, 0, j)),
                  pl.BlockSpec((None, d, tf), lambda i, j: (layer, 0, nf + j)),
                  pl.BlockSpec((None, tf, d), lambda i, j: (layer, j, 0)), vec_spec],
        out_specs=x_spec,
        out_shape=jax.ShapeDtypeStruct((m, d), F32),
        scratch_shapes=[pltpu.VMEM((tm, d), BF16)],
        compiler_params=_params(("parallel", "arbitrary"), est),
        name=name,
    )(x, g.reshape(1, d), w_in, w_in, w_out, g_final.reshape(1, d))


def kernel(x_prompt, x_sample, mem_prompt, cache_fox_k, cache_fox_v, cache_fox_logf, cache_mem_k, cache_mem_v, state_lru_h, state_lru_conv, norm_mix, norm_mem, norm_xattn, norm_ffn, norm_final, fox_w_in, fox_b_f, fox_w_out, lru_w_in, lru_conv_w, lru_conv_b, lru_w_ga, lru_b_a, lru_w_gx, lru_b_x, lru_lambda, lru_w_out, xattn_w_q, xattn_w_kv, xattn_w_o, ffn_w_in, ffn_w_out):
    b, s_len, d = x_prompt.shape
    bd, t_dec, _ = x_sample.shape
    depth = norm_mix.shape[0]
    heads = fox_b_f.shape[1]
    past = cache_fox_k.shape[2]
    n_mem, mem_heads = cache_mem_k.shape[2], cache_mem_k.shape[3]
    mem_w = mem_heads * cache_mem_k.shape[4]
    taps = lru_conv_w.shape[1]
    assert taps - 1 <= SUBLANES <= min(s_len, t_dec) and heads <= LANES

    xp = x_prompt.reshape(b * s_len, d)
    xs = x_sample.reshape(bd * t_dec, d)
    mem = mem_prompt.reshape(b * n_mem, d)
    pk, pv, plf, pmk, pmv, ph, pc = [], [], [], [], [], [], []
    sk, sv, slf, sh, sc = [], [], [], [], []

    fox_in, fox_out = fox_w_in.astype(BF16), fox_w_out.astype(BF16)
    lru_in, lru_out = lru_w_in.astype(BF16), lru_w_out.astype(BF16)
    lru_ga, lru_gx = lru_w_ga.astype(BF16), lru_w_gx.astype(BF16)
    xq, xkv, xo = xattn_w_q.astype(BF16), xattn_w_kv.astype(BF16), xattn_w_o.astype(BF16)
    ffn_in, ffn_out = ffn_w_in.astype(BF16), ffn_w_out.astype(BF16)

    for i in range(depth):
        j = i // 2
        if i % 2 == 0:
            w_f = jnp.pad(fox_in[j, :, 3 * d:], ((0, 0), (0, LANES - heads)))
            b_f = jnp.pad(fox_b_f[j], (0, LANES - heads)).reshape(1, LANES)
            q_scale = (d // heads) ** -0.5 * LOG2E
            proj = functools.partial(_norm_proj, g=norm_mix[i], w=fox_in, layer=j, ns=d, dtypes=[BF16, F32, F32],
                                     extra=(w_f, b_f, LANES, _log_sigmoid))
            proj_p = functools.partial(proj, acts=[lambda q: q * q_scale, _identity, _identity])
            proj_s = functools.partial(proj, acts=[_identity] * 3)

            qp, kp, vp, lfp = proj_p(xp, name=f"fox_proj_p{i}")
            c_p = _cumsum_rows(lfp.reshape(b, s_len, LANES), name=f"fox_cumsum_p{i}")
            attn = _fox_attn(qp.reshape(b, s_len, d), kp.reshape(b, s_len, d), vp.reshape(b, s_len, d), c_p, heads,
                             name=f"fox_attn_p{i}")
            xp = _matmul_res(attn.reshape(b * s_len, d), fox_out, j, xp, name=f"fox_out_p{i}")
            pk.append(kp.reshape(b, s_len, heads, d // heads))
            pv.append(vp.reshape(b, s_len, heads, d // heads))
            plf.append(lfp[:, :heads].reshape(b, s_len, heads))

            qs, ks, vs, lfs = proj_s(xs, name=f"fox_proj_s{i}")
            lfs = lfs[:, :heads]
            lf_all = jnp.concatenate([cache_fox_logf[j], lfs.reshape(bd, t_dec, heads)], axis=1)
            total = past + t_dec
            padded = -(-total // LANES) * LANES
            lf_t = jnp.pad(lf_all.transpose(0, 2, 1).reshape(bd * heads, total), ((0, 0), (0, padded - total)))
            c_s = _cumsum_lanes(lf_t, name=f"fox_cumsum_s{i}")
            c_s = c_s.reshape(bd, heads, padded).transpose(0, 2, 1)
            rows = t_dec * heads
            attn = _fox_dec_attn(qs.reshape(bd, rows, d // heads), ks.reshape(bd, rows, d // heads),
                                 vs.reshape(bd, rows, d // heads), cache_fox_k, cache_fox_v, j,
                                 c_s[:, :past].reshape(bd, 1, past * heads), c_s[:, past:total].reshape(bd, 1, rows),
                                 name=f"fox_attn_s{i}")
            xs = _matmul_res(attn.reshape(bd * t_dec, d), fox_out, j, xs, name=f"fox_out_s{i}")
            sk.append(ks.reshape(bd, t_dec, heads, d // heads))
            sv.append(vs.reshape(bd, t_dec, heads, d // heads))
            slf.append(lfs.reshape(bd, t_dec, heads))
        else:
            width = lru_in.shape[2] // 2
            proj = functools.partial(_norm_proj, g=norm_mix[i], w=lru_in, layer=j, ns=width,
                                     acts=[jax.nn.gelu, _identity], dtypes=[BF16, F32])
            scan = functools.partial(_lru_scan, conv_w=lru_conv_w[j], conv_b=lru_conv_b[j], w_ga=lru_ga, b_a=lru_b_a[j],
                                     w_gx=lru_gx, b_x=lru_b_x[j], lam=lru_lambda[j], layer=j)

            gate, u = proj(xp, name=f"lru_proj_p{i}")
            y, h8, u8 = scan(u.reshape(b, s_len, width), gate.reshape(b, s_len, width),
                             h0=jnp.zeros((b, 1, width), F32), buf0=jnp.zeros((b, SUBLANES, width), F32),
                             name=f"lru_scan_p{i}")
            xp = _matmul_res(y.reshape(b * s_len, width), lru_out, j, xp, name=f"lru_out_p{i}")
            ph.append(h8[:, SUBLANES - 1])
            pc.append(u8[:, SUBLANES - (taps - 1):])

            gate, u = proj(xs, name=f"lru_proj_s{i}")
            buf0 = jnp.pad(state_lru_conv[j], ((0, 0), (SUBLANES - (taps - 1), 0), (0, 0)))
            y, h8, u8 = scan(u.reshape(bd, t_dec, width), gate.reshape(bd, t_dec, width),
                             h0=state_lru_h[j].reshape(bd, 1, width), buf0=buf0, name=f"lru_scan_s{i}")
            xs = _matmul_res(y.reshape(bd * t_dec, width), lru_out, j, xs, name=f"lru_out_s{i}")
            sh.append(h8[:, SUBLANES - 1])
            sc.append(u8[:, SUBLANES - (taps - 1):])

        mk, mv = _norm_proj(mem, norm_mem[i], xkv, i, mem_w, [_identity] * 2, [F32, F32], name=f"mem_kv{i}")
        pmk.append(mk.reshape(b, n_mem, mem_heads, mem_w // mem_heads))
        pmv.append(mv.reshape(b, n_mem, mem_heads, mem_w // mem_heads))
        xp = _xattn(xp.reshape(b, s_len, d), norm_xattn[i], xq, mk.reshape(b, n_mem, mem_w), mv.reshape(b, n_mem, mem_w),
                    xo, i, mem_heads, name=f"xattn_p{i}").reshape(b * s_len, d)
        xs = _xattn(xs.reshape(bd, t_dec, d), norm_xattn[i], xq, cache_mem_k[i].reshape(bd, n_mem, mem_w),
                    cache_mem_v[i].reshape(bd, n_mem, mem_w), xo, i, mem_heads, name=f"xattn_s{i}").reshape(bd * t_dec, d)

        last = i == depth - 1
        xp = _ffn(xp, norm_ffn[i], ffn_in, ffn_out, i, norm_final, final_norm=last, name=f"ffn_p{i}")
        xs = _ffn(xs, norm_ffn[i], ffn_in, ffn_out, i, norm_final, final_norm=last, name=f"ffn_s{i}")

    return (xp.reshape(b, s_len, d), xs.reshape(bd, t_dec, d),
            jnp.stack(pk), jnp.stack(pv), jnp.stack(plf), jnp.stack(pmk), jnp.stack(pmv),
            jnp.stack(ph), jnp.stack(pc),
            jnp.stack(sk), jnp.stack(sv), jnp.stack(slf), jnp.stack(sh), jnp.stack(sc))
```

```python
import functools

import jax
import jax.numpy as jnp
from jax import lax
from jax.experimental import pallas as pl
from jax.experimental.pallas import tpu as pltpu

F32 = jnp.float32
BF16 = jnp.bfloat16

EPS = 1e-6
LOG2E = 1.4426950408889634
TINY = 1e-30
LRU_C = 8.0
LANES = 128
SUBLANES = 8
ONES_ROWS = 16
NORM_CHUNKS = 4
V7X_VMEM_BUDGET = 56 * 1024 * 1024
VMEM_FLOOR = 16 * 1024 * 1024
VMEM_SLACK = 2

TILES = dict(proj_rows=1024, proj_cols=512, out_rows=512, out_cols=512, cumsum_rows=256, attn_block=512,
             cache_rows=256, scan_rows=512, xattn_rows=1024, ffn_rows=1024, ffn_cols=512)


def _pick_tile(n, pref, align):
    if n <= pref:
        return n
    t = (pref // align) * align
    while t >= align:
        if n % t == 0:
            return t
        t -= align
    raise ValueError(f"no {align}-aligned tile of {n} below {pref}")


def _params(semantics, vmem_estimate):
    limit = int(min(V7X_VMEM_BUDGET, max(VMEM_SLACK * vmem_estimate, VMEM_FLOOR)))
    return pltpu.CompilerParams(dimension_semantics=semantics, vmem_limit_bytes=limit)


def _rms(xf, g):
    return xf * lax.rsqrt(jnp.mean(xf * xf, axis=-1, keepdims=True) + EPS) * g


def _identity(x):
    return x


def _row_chunks(rows):
    step = rows // NORM_CHUNKS if rows % (NORM_CHUNKS * 2 * SUBLANES) == 0 else rows
    return [slice(r0, r0 + step) for r0 in range(0, rows, step)]


def _softplus(x):
    return jnp.maximum(x, 0.0) + jnp.log1p(jnp.exp(-jnp.abs(x)))


def _log_sigmoid(x):
    return -_softplus(-x)


def _norm_proj_kernel(*refs, acts, extra_act):
    n_sec = len(acts)
    x_ref, g_ref = refs[0], refs[1]
    w_refs = refs[2:2 + n_sec]
    pos = 2 + n_sec
    if extra_act is not None:
        we_ref, be_ref = refs[pos], refs[pos + 1]
        pos += 2
    o_refs = refs[pos:pos + n_sec]
    pos += n_sec
    if extra_act is not None:
        e_ref = refs[pos]
        pos += 1
    h_ref = refs[pos]

    def project(h, rows):
        for w_ref, o_ref, act in zip(w_refs, o_refs, acts):
            o_ref[rows, :] = act(jnp.dot(h, w_ref[...], preferred_element_type=F32)).astype(o_ref.dtype)

    first = pl.program_id(1) == 0

    @pl.when(first)
    def _():
        for rows in _row_chunks(x_ref.shape[0]):
            h = _rms(x_ref[rows, :], g_ref[...]).astype(BF16)
            h_ref[rows, :] = h
            if extra_act is not None:
                z = jnp.dot(h, we_ref[...], preferred_element_type=F32) + be_ref[...]
                e_ref[rows, :] = extra_act(z)[:, :e_ref.shape[1]]
            project(h, rows)

    @pl.when(jnp.logical_not(first))
    def _():
        project(h_ref[...], slice(None))


def _norm_proj(x, g, w, layer, ns, acts, dtypes, *, name, extra=None, tm_pref=TILES["proj_rows"], tn_pref=TILES["proj_cols"]):
    m, d = x.shape
    n_sec = len(acts)
    tm = _pick_tile(m, tm_pref, SUBLANES)
    tn = _pick_tile(ns, tn_pref, LANES)
    tiles = ns // tn
    in_specs = [pl.BlockSpec((tm, d), lambda i, j: (i, 0)),
                pl.BlockSpec((1, d), lambda i, j: (0, 0))]
    in_specs += [pl.BlockSpec((None, d, tn), lambda i, j, s=s: (layer, 0, s * tiles + j)) for s in range(n_sec)]
    args = [x, g.reshape(1, d)] + [w] * n_sec
    out_specs = [pl.BlockSpec((tm, tn), lambda i, j: (i, j)) for _ in range(n_sec)]
    out_shape = [jax.ShapeDtypeStruct((m, ns), dt) for dt in dtypes]
    extra_act = None
    if extra is not None:
        w_e, b_e, n_e, extra_act = extra
        in_specs += [pl.BlockSpec(w_e.shape, lambda i, j: (0, 0)), pl.BlockSpec(b_e.shape, lambda i, j: (0, 0))]
        args += [w_e, b_e]
        out_specs.append(pl.BlockSpec((tm, n_e), lambda i, j: (i, 0)))
        out_shape.append(jax.ShapeDtypeStruct((m, n_e), F32))
    est = 2 * tm * d * 4 + tm * d * 2 + n_sec * 2 * (d * tn * 2 + tm * tn * 4) + 4 * tm * tn * 4
    return pl.pallas_call(
        functools.partial(_norm_proj_kernel, acts=tuple(acts), extra_act=extra_act),
        grid=(m // tm, tiles),
        in_specs=in_specs, out_specs=out_specs, out_shape=out_shape,
        scratch_shapes=[pltpu.VMEM((tm, d), BF16)],
        compiler_params=_params(("parallel", "arbitrary"), est),
        name=name,
    )(*args)


def _matmul_res_kernel(a_ref, w_ref, r_ref, o_ref, *, tn):
    a = a_ref[...]
    for n0 in range(0, o_ref.shape[1], tn):
        o_ref[:, n0:n0 + tn] = r_ref[:, n0:n0 + tn] + jnp.dot(a, w_ref[:, n0:n0 + tn], preferred_element_type=F32)


def _matmul_res(a, w, layer, res, *, name, tm_pref=TILES["out_rows"], tn_pref=TILES["out_cols"]):
    m, k = a.shape
    n = w.shape[2]
    tm = _pick_tile(m, tm_pref, SUBLANES)
    tn = _pick_tile(n, tn_pref, LANES)
    est = 2 * (tm * k * 2 + k * n * 2 + 2 * tm * n * 4) + 2 * tm * tn * 4
    return pl.pallas_call(
        functools.partial(_matmul_res_kernel, tn=tn),
        grid=(m // tm,),
        in_specs=[pl.BlockSpec((tm, k), lambda i: (i, 0)),
                  pl.BlockSpec((None, k, n), lambda i: (layer, 0, 0)),
                  pl.BlockSpec((tm, n), lambda i: (i, 0))],
        out_specs=pl.BlockSpec((tm, n), lambda i: (i, 0)),
        out_shape=jax.ShapeDtypeStruct((m, n), F32),
        compiler_params=_params(("parallel",), est),
        name=name,
    )(a, w, res)


def _cumsum_kernel(x_ref, o_ref, carry_ref):
    @pl.when(pl.program_id(0) == 0)
    def _():
        carry_ref[...] = jnp.zeros_like(carry_ref)

    x = x_ref[...]
    n = x.shape[1]
    upper = (lax.broadcasted_iota(jnp.int32, (n, n), 0) <= lax.broadcasted_iota(jnp.int32, (n, n), 1)).astype(BF16)
    hi = x.astype(BF16)
    rest = x - hi.astype(F32)
    mid = rest.astype(BF16)
    lo = (rest - mid.astype(F32)).astype(BF16)
    local = (jnp.dot(hi, upper, preferred_element_type=F32) + jnp.dot(mid, upper, preferred_element_type=F32)
             + jnp.dot(lo, upper, preferred_element_type=F32))
    out = local + carry_ref[...]
    o_ref[...] = out
    carry_ref[...] = jnp.broadcast_to(out[:, n - 1:n], out.shape)


def _cumsum_lanes(x, *, name):
    r, l = x.shape
    return pl.pallas_call(
        _cumsum_kernel,
        grid=(l // LANES,),
        in_specs=[pl.BlockSpec((r, LANES), lambda t: (0, t))],
        out_specs=pl.BlockSpec((r, LANES), lambda t: (0, t)),
        out_shape=jax.ShapeDtypeStruct((r, l), F32),
        scratch_shapes=[pltpu.VMEM((r, LANES), F32)],
        compiler_params=_params(("arbitrary",), 8 * r * LANES * 4),
        name=name,
    )(x)


def _cumsum_rows_kernel(x_ref, o_ref, carry_ref):
    @pl.when(pl.program_id(1) == 0)
    def _():
        carry_ref[...] = jnp.zeros_like(carry_ref)

    x = x_ref[0]
    n = x.shape[0]
    lower = (lax.broadcasted_iota(jnp.int32, (n, n), 0) >= lax.broadcasted_iota(jnp.int32, (n, n), 1)).astype(BF16)
    hi = x.astype(BF16)
    rest = x - hi.astype(F32)
    mid = rest.astype(BF16)
    lo = (rest - mid.astype(F32)).astype(BF16)
    local = (jnp.dot(lower, hi, preferred_element_type=F32) + jnp.dot(lower, mid, preferred_element_type=F32)
             + jnp.dot(lower, lo, preferred_element_type=F32))
    out = local + carry_ref[...]
    o_ref[0] = out
    carry_ref[...] = out[n - 1:n, :]


def _cumsum_rows(x, *, name, tile_pref=TILES["cumsum_rows"]):
    b, s_len, w = x.shape
    tile = _pick_tile(s_len, tile_pref, SUBLANES)
    spec = pl.BlockSpec((1, tile, w), lambda bi, ti: (bi, ti, 0))
    return pl.pallas_call(
        _cumsum_rows_kernel,
        grid=(b, s_len // tile),
        in_specs=[spec], out_specs=spec,
        out_shape=jax.ShapeDtypeStruct((b, s_len, w), F32),
        scratch_shapes=[pltpu.VMEM((1, w), F32)],
        compiler_params=_params(("parallel", "arbitrary"), 8 * tile * w * 4 + 4 * tile * tile),
        name=name,
    )(x)


def _fox_attn_kernel(q_ref, k_ref, v_ref, c_ref, o_ref, kaug_ref, vt_ref, ta_ref, tb_ref, pa_ref, pb_ref,
                     acc_ref, m_ref, qt_ref, *, tq, heads):
    s_len, dh = q_ref.shape[1], q_ref.shape[2]
    nk = s_len // tq
    h = pl.program_id(1)

    @pl.when(h == 0)
    def _():
        head_lane = lax.broadcasted_iota(jnp.int32, c_ref.shape[1:], 1) < heads
        ckp = jnp.where(head_lane, c_ref[0] * (-LOG2E), 0.0)
        hi = ckp.astype(BF16).astype(F32)
        rest = ckp - hi
        mid = rest.astype(BF16).astype(F32)
        lo = (rest - mid).astype(BF16).astype(F32)
        aug = hi + pltpu.roll(mid, heads, axis=1) + pltpu.roll(lo, 2 * heads, axis=1)
        kaug_ref[:, dh:] = aug.astype(BF16)

    kaug_ref[:, :dh] = k_ref[0].astype(BF16)
    for kj in range(nk):
        vt_ref[kj, :dh] = v_ref[0, kj * tq:(kj + 1) * tq, :].T.astype(BF16)
        vt_ref[kj, dh:] = (lax.broadcasted_iota(jnp.int32, (ONES_ROWS, tq), 0) == 0).astype(BF16)

    row = lax.broadcasted_iota(jnp.int32, (LANES, tq), 0)
    pick_t = ((row == h) | (row == heads + h) | (row == 2 * heads + h)).astype(BF16)
    for qi in range(nk):
        qt_ref[qi, :dh] = q_ref[0, qi * tq:(qi + 1) * tq, :].T
        qt_ref[qi, dh:] = pick_t
    t_refs, p_refs = (ta_ref, tb_ref), (pa_ref, pb_ref)

    m_ref[...] = jnp.full(m_ref.shape, -jnp.inf, F32)
    acc_ref[...] = jnp.zeros(acc_ref.shape, F32)

    def scores(pair, masked, t_ref):
        qa, ka = pair
        t = jnp.dot(kaug_ref[ka * tq:(ka + 1) * tq, :], qt_ref[qa], preferred_element_type=F32)
        if masked:
            causal = lax.broadcasted_iota(jnp.int32, (tq, tq), 0) <= lax.broadcasted_iota(jnp.int32, (tq, tq), 1)
            t = jnp.where(causal, t, -jnp.inf)
        t_ref[...] = t
        return jnp.max(t, axis=0, keepdims=True)

    def softmax_update(pair, t_ref, p_ref, m_blk):
        qa, _ = pair
        m_prev = m_ref[qa]
        m_new = jnp.maximum(m_prev, m_blk)
        alpha = jnp.exp2(m_prev - m_new)
        p_ref[...] = jnp.exp2(t_ref[...] - m_new).astype(BF16)
        m_ref[qa] = m_new
        return alpha

    def accumulate(pair, p_ref, alpha):
        qa, ka = pair
        acc_ref[qa] = alpha * acc_ref[qa] + jnp.dot(vt_ref[ka], p_ref[...], preferred_element_type=F32)

    pairs = [(qa, qa) for qa in range(nk)] + [(qa, ka) for qa in range(nk) for ka in range(qa)]
    m_blk = alpha = None
    for e in range(len(pairs) + 2):
        if e >= 2:
            accumulate(pairs[e - 2], p_refs[e % 2], alpha)
        if 1 <= e <= len(pairs):
            alpha = softmax_update(pairs[e - 1], t_refs[(e - 1) % 2], p_refs[(e - 1) % 2], m_blk)
        if e < len(pairs):
            m_blk = scores(pairs[e], e < nk, t_refs[e % 2])

    for qi in range(nk):
        out_t = acc_ref[qi, :dh] / acc_ref[qi, dh:dh + 1]
        o_ref[0, qi * tq:(qi + 1) * tq, :] = out_t.T.astype(o_ref.dtype)


def _fox_attn(q, k, v, c, heads, *, name, tq_pref=TILES["attn_block"]):
    b, s_len, d = q.shape
    dh = d // heads
    assert 3 * heads <= LANES
    tq = _pick_tile(s_len, tq_pref, LANES)
    nk = s_len // tq
    qkv_spec = pl.BlockSpec((1, s_len, dh), lambda bi, hi: (bi, 0, hi))
    est = (2 * s_len * dh * (2 + 4 + 4 + 2) + 2 * s_len * LANES * 4 + s_len * (2 * dh + LANES) * 2
           + s_len * (dh + ONES_ROWS + 1) * 4 + s_len * (dh + LANES) * 2 + 3 * tq * tq * 4 + 8 * tq * tq * 4)
    return pl.pallas_call(
        functools.partial(_fox_attn_kernel, tq=tq, heads=heads),
        grid=(b, heads),
        in_specs=[qkv_spec, qkv_spec, qkv_spec,
                  pl.BlockSpec((1, s_len, LANES), lambda bi, hi: (bi, 0, 0))],
        out_specs=qkv_spec,
        out_shape=jax.ShapeDtypeStruct((b, s_len, d), BF16),
        scratch_shapes=[pltpu.VMEM((s_len, dh + LANES), BF16), pltpu.VMEM((nk, dh + ONES_ROWS, tq), BF16),
                        pltpu.VMEM((tq, tq), F32), pltpu.VMEM((tq, tq), F32),
                        pltpu.VMEM((tq, tq), BF16), pltpu.VMEM((tq, tq), BF16),
                        pltpu.VMEM((nk, dh + ONES_ROWS, tq), F32), pltpu.VMEM((nk, 1, tq), F32),
                        pltpu.VMEM((nk, dh + LANES, tq), BF16)],
        compiler_params=_params(("parallel", "arbitrary"), est),
        name=name,
    )(q, k, v, c)


def _fox_dec_kernel(q_ref, kn_ref, vn_ref, kc_ref, vc_ref, ckc_ref, ckn_ref, o_ref, m_scr, l_scr, acc_scr, mask_scr,
                    *, heads):
    step = pl.program_id(1)
    q = q_ref[0]
    rows, dh = q.shape
    tp = kc_ref.shape[1]
    shift = heads.bit_length() - 1

    def head_mask(keys, causal):
        r = lax.broadcasted_iota(jnp.int32, (rows, keys), 0)
        j = lax.broadcasted_iota(jnp.int32, (rows, keys), 1)
        visible = ((r ^ j) & (heads - 1)) == 0
        if causal:
            visible = visible & ((j >> shift) <= (r >> shift))
        return jnp.where(visible, 0.0, -jnp.inf)

    @pl.when(step == 0)
    def _():
        m_scr[...] = jnp.full(m_scr.shape, -jnp.inf, F32)
        l_scr[...] = jnp.zeros(l_scr.shape, F32)
        acc_scr[...] = jnp.zeros(acc_scr.shape, F32)
        mask_scr[...] = head_mask(tp * heads, False)

    def absorb(k_flat, v_flat, bias):
        t = lax.dot_general(q, k_flat.astype(BF16), (((1,), (1,)), ((), ())), preferred_element_type=F32) + bias
        m_prev = m_scr[...]
        m_new = jnp.maximum(m_prev, jnp.max(t, axis=-1, keepdims=True))
        p = jnp.exp2(t - m_new)
        alpha = jnp.exp2(m_prev - m_new)
        l_scr[...] = alpha * l_scr[...] + jnp.sum(p, axis=-1, keepdims=True)
        acc_scr[...] = alpha * acc_scr[...] + jnp.dot(p.astype(BF16), v_flat.astype(BF16), preferred_element_type=F32)
        m_scr[...] = m_new

    absorb(kc_ref[0].reshape(tp * heads, dh), vc_ref[0].reshape(tp * heads, dh), mask_scr[...] - ckc_ref[0] * LOG2E)

    @pl.when(step == pl.num_programs(1) - 1)
    def _():
        absorb(kn_ref[0], vn_ref[0], head_mask(rows, True) - ckn_ref[0] * LOG2E)
        o_ref[0] = (acc_scr[...] / l_scr[...]).astype(o_ref.dtype)


def _fox_dec_attn(q, k_new, v_new, k_cache, v_cache, layer, c_cache, c_new, *, name, tp_pref=TILES["cache_rows"]):
    b, rows, dh = q.shape
    _, _, p, heads, _ = k_cache.shape
    assert heads & (heads - 1) == 0
    tp = _pick_tile(p, tp_pref, SUBLANES)
    new_spec = pl.BlockSpec((1, rows, dh), lambda bi, pi: (bi, 0, 0))
    cache_spec = pl.BlockSpec((None, 1, tp, heads, dh), lambda bi, pi: (layer, bi, pi, 0, 0))
    est = 2 * 2 * tp * heads * dh * 4 + 2 * tp * heads * dh * 2 + 7 * rows * tp * heads * 4
    return pl.pallas_call(
        functools.partial(_fox_dec_kernel, heads=heads),
        grid=(b, p // tp),
        in_specs=[new_spec, new_spec, new_spec, cache_spec, cache_spec,
                  pl.BlockSpec((1, 1, tp * heads), lambda bi, pi: (bi, 0, pi)),
                  pl.BlockSpec((1, 1, rows), lambda bi, pi: (bi, 0, 0))],
        out_specs=new_spec,
        out_shape=jax.ShapeDtypeStruct((b, rows, dh), BF16),
        scratch_shapes=[pltpu.VMEM((rows, 1), F32), pltpu.VMEM((rows, 1), F32), pltpu.VMEM((rows, dh), F32),
                        pltpu.VMEM((rows, tp * heads), F32)],
        compiler_params=_params(("parallel", "arbitrary"), est),
        name=name,
    )(q, k_new, v_new, k_cache, v_cache, c_cache, c_new)


def _lru_kernel(u_ref, gate_ref, cw_ref, cb_ref, wa_ref, ba_ref, wx_ref, bx_ref, lam_ref, h0_ref, buf0_ref,
                y_ref, hl_ref, ul_ref, h_scr, tail_scr):
    @pl.when(pl.program_id(2) == 0)
    def _():
        h_scr[...] = h0_ref[0]
        tail_scr[...] = buf0_ref[0]

    u = u_ref[0]
    tc, bw = u.shape
    taps = cw_ref.shape[0]
    tail = tail_scr[...]
    head_rows = lax.broadcasted_iota(jnp.int32, (SUBLANES, bw), 0)

    def delayed(d):
        if d == 0:
            return u
        ru = pltpu.roll(u, d, axis=0)
        head = jnp.where(head_rows < d, pltpu.roll(tail, d, axis=0), ru[:SUBLANES])
        return head if tc == SUBLANES else jnp.concatenate([head, ru[SUBLANES:]], axis=0)

    conv = delayed(taps - 1) * cw_ref[0:1, :]
    for k in range(1, taps):
        conv = conv + delayed(taps - 1 - k) * cw_ref[k:k + 1, :]
    uc = cb_ref[...] + conv
    tail_scr[...] = u[tc - SUBLANES:, :]
    ul_ref[0] = u[tc - SUBLANES:, :]

    ucb = uc.astype(BF16)
    r = jax.nn.sigmoid(jnp.dot(ucb, wa_ref[0], preferred_element_type=F32) + ba_ref[...])
    ig = jax.nn.sigmoid(jnp.dot(ucb, wx_ref[0], preferred_element_type=F32) + bx_ref[...])
    log_a = (-LRU_C * _softplus(-lam_ref[...])) * r
    a = jnp.exp(log_a)
    th = jnp.tanh(log_a)
    decay = (-2.0 * th) / (1.0 - th)
    b = (decay * lax.rsqrt(jnp.maximum(decay, TINY))) * (ig * uc)

    groups = tc // SUBLANES
    a3 = a.reshape(groups, SUBLANES, bw)
    b3 = b.reshape(groups, SUBLANES, bw)
    sub = lax.broadcasted_iota(jnp.int32, (groups, SUBLANES, bw), 1)
    d = 1
    while d < SUBLANES:
        valid = sub >= d
        b3 = jnp.where(valid, a3 * pltpu.roll(b3, d, axis=1) + b3, b3)
        a3 = jnp.where(valid, a3 * pltpu.roll(a3, d, axis=1), a3)
        d *= 2
    h = h_scr[...]
    hs = []
    for gi in range(groups):
        hg = a3[gi] * h + b3[gi]
        hs.append(hg)
        h = hg[SUBLANES - 1:SUBLANES, :]
    h_scr[...] = h
    hl_ref[0] = hs[-1]
    hs = hs[0] if groups == 1 else jnp.concatenate(hs, axis=0)
    y_ref[0] = (hs * gate_ref[0].astype(F32)).astype(y_ref.dtype)


def _lru_scan(u, gate, conv_w, conv_b, w_ga, b_a, w_gx, b_x, lam, layer, h0, buf0, *, name, tc_pref=TILES["scan_rows"]):
    b, t, w = u.shape
    _, nb, bw, _ = w_ga.shape
    taps = conv_w.shape[0]
    tc = _pick_tile(t, tc_pref, SUBLANES)
    seq_spec = pl.BlockSpec((1, tc, bw), lambda bi, ni, ti: (bi, ti, ni))
    row_spec = pl.BlockSpec((1, bw), lambda bi, ni, ti: (0, ni))
    blk_spec = pl.BlockSpec((None, 1, bw, bw), lambda bi, ni, ti: (layer, ni, 0, 0))
    last_spec = pl.BlockSpec((1, SUBLANES, bw), lambda bi, ni, ti: (bi, 0, ni))
    est = 2 * tc * bw * (4 + 2 + 2) + 4 * bw * bw * 2 + 24 * tc * bw * 4
    return pl.pallas_call(
        _lru_kernel,
        grid=(b, nb, t // tc),
        in_specs=[seq_spec, seq_spec,
                  pl.BlockSpec((taps, bw), lambda bi, ni, ti: (0, ni)), row_spec,
                  blk_spec, row_spec, blk_spec, row_spec, row_spec,
                  pl.BlockSpec((1, 1, bw), lambda bi, ni, ti: (bi, 0, ni)), last_spec],
        out_specs=[seq_spec, last_spec, last_spec],
        out_shape=[jax.ShapeDtypeStruct((b, t, w), BF16),
                   jax.ShapeDtypeStruct((b, SUBLANES, w), F32),
                   jax.ShapeDtypeStruct((b, SUBLANES, w), F32)],
        scratch_shapes=[pltpu.VMEM((1, bw), F32), pltpu.VMEM((SUBLANES, bw), F32)],
        compiler_params=_params(("parallel", "parallel", "arbitrary"), est),
        name=name,
    )(u, gate, conv_w, conv_b.reshape(1, w), w_ga, b_a.reshape(1, w), w_gx, b_x.reshape(1, w),
      lam.reshape(1, w), h0, buf0)


def _xattn_kernel(x_ref, g_ref, wq_ref, mk_ref, mv_ref, wo_ref, o_ref, *, heads):
    x = x_ref[0]
    h = _rms(x, g_ref[...]).astype(BF16)
    q = jnp.dot(h, wq_ref[...], preferred_element_type=F32).astype(BF16)
    dh = q.shape[1] // heads
    scale = dh ** -0.5
    outs = []
    for hd in range(heads):
        cols = slice(hd * dh, (hd + 1) * dh)
        s = lax.dot_general(q[:, cols], mk_ref[0, :, cols].astype(BF16), (((1,), (1,)), ((), ())),
                            preferred_element_type=F32) * scale
        p = jnp.exp(s - jnp.max(s, axis=-1, keepdims=True))
        l = jnp.sum(p, axis=-1, keepdims=True)
        o = jnp.dot(p.astype(BF16), mv_ref[0, :, cols].astype(BF16), preferred_element_type=F32)
        outs.append((o / l).astype(BF16))
    o_all = jnp.concatenate(outs, axis=-1)
    o_ref[0] = x + jnp.dot(o_all, wo_ref[...], preferred_element_type=F32)


def _xattn(x, g, wq, mk, mv, wo, layer, heads, *, name, tm_pref=TILES["xattn_rows"]):
    b, t, d = x.shape
    nm, mw = mk.shape[1], mk.shape[2]
    tm = _pick_tile(t, tm_pref, SUBLANES)
    x_spec = pl.BlockSpec((1, tm, d), lambda bi, ti: (bi, ti, 0))
    mem_spec = pl.BlockSpec((1, nm, mw), lambda bi, ti: (bi, 0, 0))
    est = 4 * tm * d * 4 + 4 * d * mw * 2 + 4 * nm * mw * 4 + 3 * tm * d * 4
    return pl.pallas_call(
        functools.partial(_xattn_kernel, heads=heads),
        grid=(b, t // tm),
        in_specs=[x_spec, pl.BlockSpec((1, d), lambda bi, ti: (0, 0)),
                  pl.BlockSpec((None, d, mw), lambda bi, ti: (layer, 0, 0)), mem_spec, mem_spec,
                  pl.BlockSpec((None, mw, d), lambda bi, ti: (layer, 0, 0))],
        out_specs=x_spec,
        out_shape=jax.ShapeDtypeStruct((b, t, d), F32),
        compiler_params=_params(("parallel", "parallel"), est),
        name=name,
    )(x, g.reshape(1, d), wq, mk, mv, wo)


def _ffn_kernel(x_ref, g_ref, wg_ref, wu_ref, wo_ref, gf_ref, o_ref, h_ref, *, final_norm):
    j = pl.program_id(1)

    def mix(h):
        gate = jnp.dot(h, wg_ref[...], preferred_element_type=F32)
        up = jnp.dot(h, wu_ref[...], preferred_element_type=F32)
        act = (jax.nn.silu(gate) * up).astype(BF16)
        return jnp.dot(act, wo_ref[...], preferred_element_type=F32)

    @pl.when(j == 0)
    def _():
        for rows in _row_chunks(x_ref.shape[0]):
            x = x_ref[rows, :]
            h = _rms(x, g_ref[...]).astype(BF16)
            h_ref[rows, :] = h
            o_ref[rows, :] = x + mix(h)

    @pl.when(j > 0)
    def _():
        o_ref[...] += mix(h_ref[...])

    if final_norm:
        @pl.when(j == pl.num_programs(1) - 1)
        def _():
            o_ref[...] = _rms(o_ref[...], gf_ref[...])


def _ffn(x, g, w_in, w_out, layer, g_final, *, final_norm, name, tm_pref=TILES["ffn_rows"], tf_pref=TILES["ffn_cols"]):
    m, d = x.shape
    f = w_out.shape[1]
    tm = _pick_tile(m, tm_pref, SUBLANES)
    tf = _pick_tile(f, tf_pref, LANES)
    nf = f // tf
    x_spec = pl.BlockSpec((tm, d), lambda i, j: (i, 0))
    vec_spec = pl.BlockSpec((1, d), lambda i, j: (0, 0))
    est = 4 * tm * d * 4 + tm * d * 2 + 2 * 3 * d * tf * 2 + 4 * tm * tf * 4
    return pl.pallas_call(
        functools.partial(_ffn_kernel, final_norm=final_norm),
        grid=(m // tm, nf),
        in_specs=[x_spec, vec_spec,
                  pl.BlockSpec((None, d, tf), lambda i, j: (layer, 0, j)),
                  pl.BlockSpec((None, d, tf), lambda i, j: (layer, 0, nf + j)),
                  pl.BlockSpec((None, tf, d), lambda i, j: (layer, j, 0)), vec_spec],
        out_specs=x_spec,
        out_shape=jax.ShapeDtypeStruct((m, d), F32),
        scratch_shapes=[pltpu.VMEM((tm, d), BF16)],
        compiler_params=_params(("parallel", "arbitrary"), est),
        name=name,
    )(x, g.reshape(1, d), w_in, w_in, w_out, g_final.reshape(1, d))


def kernel(x_prompt, x_sample, mem_prompt, cache_fox_k, cache_fox_v, cache_fox_logf, cache_mem_k, cache_mem_v, state_lru_h, state_lru_conv, norm_mix, norm_mem, norm_xattn, norm_ffn, norm_final, fox_w_in, fox_b_f, fox_w_out, lru_w_in, lru_conv_w, lru_conv_b, lru_w_ga, lru_b_a, lru_w_gx, lru_b_x, lru_lambda, lru_w_out, xattn_w_q, xattn_w_kv, xattn_w_o, ffn_w_in, ffn_w_out):
    b, s_len, d = x_prompt.shape
    bd, t_dec, _ = x_sample.shape
    depth = norm_mix.shape[0]
    heads = fox_b_f.shape[1]
    past = cache_fox_k.shape[2]
    n_mem, mem_heads = cache_mem_k.shape[2], cache_mem_k.shape[3]
    mem_w = mem_heads * cache_mem_k.shape[4]
    taps = lru_conv_w.shape[1]
    assert taps - 1 <= SUBLANES <= min(s_len, t_dec) and heads <= LANES

    xp = x_prompt.reshape(b * s_len, d)
    xs = x_sample.reshape(bd * t_dec, d)
    mem = mem_prompt.reshape(b * n_mem, d)
    pk, pv, plf, pmk, pmv, ph, pc = [], [], [], [], [], [], []
    sk, sv, slf, sh, sc = [], [], [], [], []

    fox_in, fox_out = fox_w_in.astype(BF16), fox_w_out.astype(BF16)
    lru_in, lru_out = lru_w_in.astype(BF16), lru_w_out.astype(BF16)
    lru_ga, lru_gx = lru_w_ga.astype(BF16), lru_w_gx.astype(BF16)
    xq, xkv, xo = xattn_w_q.astype(BF16), xattn_w_kv.astype(BF16), xattn_w_o.astype(BF16)
    ffn_in, ffn_out = ffn_w_in.astype(BF16), ffn_w_out.astype(BF16)

    for i in range(depth):
        j = i // 2
        if i % 2 == 0:
            w_f = jnp.pad(fox_in[j, :, 3 * d:], ((0, 0), (0, LANES - heads)))
            b_f = jnp.pad(fox_b_f[j], (0, LANES - heads)).reshape(1, LANES)
            q_scale = (d // heads) ** -0.5 * LOG2E
            proj = functools.partial(_norm_proj, g=norm_mix[i], w=fox_in, layer=j, ns=d, dtypes=[BF16, F32, F32],
                                     acts=[lambda q: q * q_scale, _identity, _identity],
                                     extra=(w_f, b_f, LANES, _log_sigmoid))

            qp, kp, vp, lfp = proj(xp, name=f"fox_proj_p{i}")
            c_p = _cumsum_rows(lfp.reshape(b, s_len, LANES), name=f"fox_cumsum_p{i}")
            attn = _fox_attn(qp.reshape(b, s_len, d), kp.reshape(b, s_len, d), vp.reshape(b, s_len, d), c_p, heads,
                             name=f"fox_attn_p{i}")
            xp = _matmul_res(attn.reshape(b * s_len, d), fox_out, j, xp, name=f"fox_out_p{i}")
            pk.append(kp.reshape(b, s_len, heads, d // heads))
            pv.append(vp.reshape(b, s_len, heads, d // heads))
            plf.append(lfp[:, :heads].reshape(b, s_len, heads))

            qs, ks, vs, lfs = proj(xs, name=f"fox_proj_s{i}")
            lfs = lfs[:, :heads]
            lf_all = jnp.concatenate([cache_fox_logf[j], lfs.reshape(bd, t_dec, heads)], axis=1)
            total = past + t_dec
            padded = -(-total // LANES) * LANES
            lf_t = jnp.pad(lf_all.transpose(0, 2, 1).reshape(bd * heads, total), ((0, 0), (0, padded - total)))
            c_s = _cumsum_lanes(lf_t, name=f"fox_cumsum_s{i}")
            c_s = c_s.reshape(bd, heads, padded).transpose(0, 2, 1)
            rows = t_dec * heads
            attn = _fox_dec_attn(qs.reshape(bd, rows, d // heads), ks.reshape(bd, rows, d // heads),
                                 vs.reshape(bd, rows, d // heads), cache_fox_k, cache_fox_v, j,
                                 c_s[:, :past].reshape(bd, 1, past * heads), c_s[:, past:total].reshape(bd, 1, rows),
                                 name=f"fox_attn_s{i}")
            xs = _matmul_res(attn.reshape(bd * t_dec, d), fox_out, j, xs, name=f"fox_out_s{i}")
            sk.append(ks.reshape(bd, t_dec, heads, d // heads))
            sv.append(vs.reshape(bd, t_dec, heads, d // heads))
            slf.append(lfs.reshape(bd, t_dec, heads))
        else:
            width = lru_in.shape[2] // 2
            proj = functools.partial(_norm_proj, g=norm_mix[i], w=lru_in, layer=j, ns=width,
                                     acts=[jax.nn.gelu, _identity], dtypes=[BF16, F32])
            scan = functools.partial(_lru_scan, conv_w=lru_conv_w[j], conv_b=lru_conv_b[j], w_ga=lru_ga, b_a=lru_b_a[j],
                                     w_gx=lru_gx, b_x=lru_b_x[j], lam=lru_lambda[j], layer=j)

            gate, u = proj(xp, name=f"lru_proj_p{i}")
            y, h8, u8 = scan(u.reshape(b, s_len, width), gate.reshape(b, s_len, width),
                             h0=jnp.zeros((b, 1, width), F32), buf0=jnp.zeros((b, SUBLANES, width), F32),
                             name=f"lru_scan_p{i}")
            xp = _matmul_res(y.reshape(b * s_len, width), lru_out, j, xp, name=f"lru_out_p{i}")
            ph.append(h8[:, SUBLANES - 1])
            pc.append(u8[:, SUBLANES - (taps - 1):])

            gate, u = proj(xs, name=f"lru_proj_s{i}")
            buf0 = jnp.pad(state_lru_conv[j], ((0, 0), (SUBLANES - (taps - 1), 0), (0, 0)))
            y, h8, u8 = scan(u.reshape(bd, t_dec, width), gate.reshape(bd, t_dec, width),
                             h0=state_lru_h[j].reshape(bd, 1, width), buf0=buf0, name=f"lru_scan_s{i}")
            xs = _matmul_res(y.reshape(bd * t_dec, width), lru_out, j, xs, name=f"lru_out_s{i}")
            sh.append(h8[:, SUBLANES - 1])
            sc.append(u8[:, SUBLANES - (taps - 1):])

        mk, mv = _norm_proj(mem, norm_mem[i], xkv, i, mem_w, [_identity] * 2, [F32, F32], name=f"mem_kv{i}")
        pmk.append(mk.reshape(b, n_mem, mem_heads, mem_w // mem_heads))
        pmv.append(mv.reshape(b, n_mem, mem_heads, mem_w // mem_heads))
        xp = _xattn(xp.reshape(b, s_len, d), norm_xattn[i], xq, mk.reshape(b, n_mem, mem_w), mv.reshape(b, n_mem, mem_w),
                    xo, i, mem_heads, name=f"xattn_p{i}").reshape(b * s_len, d)
        xs = _xattn(xs.reshape(bd, t_dec, d), norm_xattn[i], xq, cache_mem_k[i].reshape(bd, n_mem, mem_w),
                    cache_mem_v[i].reshape(bd, n_mem, mem_w), xo, i, mem_heads, name=f"xattn_s{i}").reshape(bd * t_dec, d)

        last = i == depth - 1
        xp = _ffn(xp, norm_ffn[i], ffn_in, ffn_out, i, norm_final, final_norm=last, name=f"ffn_p{i}")
        xs = _ffn(xs, norm_ffn[i], ffn_in, ffn_out, i, norm_final, final_norm=last, name=f"ffn_s{i}")

    return (xp.reshape(b, s_len, d), xs.reshape(bd, t_dec, d),
            jnp.stack(pk), jnp.stack(pv), jnp.stack(plf), jnp.stack(pmk), jnp.stack(pmv),
            jnp.stack(ph), jnp.stack(pc),
            jnp.stack(sk), jnp.stack(sv), jnp.stack(slf), jnp.stack(sh), jnp.stack(sc))
```

```python
import functools

import jax
import jax.numpy as jnp
from jax import lax
from jax.experimental import pallas as pl
from jax.experimental.pallas import tpu as pltpu

F32 = jnp.float32
BF16 = jnp.bfloat16

EPS = 1e-6
LOG2E = 1.4426950408889634
TINY = 1e-30
LRU_C = 8.0
LANES = 128
SUBLANES = 8
ONES_ROWS = 16
NORM_CHUNKS = 4
V7X_VMEM_BUDGET = 56 * 1024 * 1024
VMEM_FLOOR = 16 * 1024 * 1024
VMEM_SLACK = 2

TILES = dict(proj_rows=1024, proj_cols=512, cumsum_rows=256, attn_block=512,
             cache_rows=256, scan_rows=512, xattn_rows=512, ffn_rows=1024, ffn_cols=512)


def _pick_tile(n, pref, align):
    if n <= pref:
        return n
    t = (pref // align) * align
    while t >= align:
        if n % t == 0:
            return t
        t -= align
    raise ValueError(f"no {align}-aligned tile of {n} below {pref}")


def _params(semantics, vmem_estimate):
    limit = int(min(V7X_VMEM_BUDGET, max(VMEM_SLACK * vmem_estimate, VMEM_FLOOR)))
    return pltpu.CompilerParams(dimension_semantics=semantics, vmem_limit_bytes=limit)


def _rms(xf, g):
    return xf * lax.rsqrt(jnp.mean(xf * xf, axis=-1, keepdims=True) + EPS) * g


def _identity(x):
    return x


def _row_chunks(rows):
    step = rows // NORM_CHUNKS if rows % (NORM_CHUNKS * 2 * SUBLANES) == 0 else rows
    return [slice(r0, r0 + step) for r0 in range(0, rows, step)]


def _softplus(x):
    return jnp.maximum(x, 0.0) + jnp.log1p(jnp.exp(-jnp.abs(x)))


def _log_sigmoid(x):
    return -_softplus(-x)


def _norm_proj_kernel(*refs, acts, extra_act):
    n_sec = len(acts)
    x_ref, g_ref = refs[0], refs[1]
    w_refs = refs[2:2 + n_sec]
    pos = 2 + n_sec
    if extra_act is not None:
        we_ref, be_ref = refs[pos], refs[pos + 1]
        pos += 2
    o_refs = refs[pos:pos + n_sec]
    pos += n_sec
    if extra_act is not None:
        e_ref = refs[pos]
        pos += 1
    h_ref = refs[pos]

    def project(h, rows):
        for w_ref, o_ref, act in zip(w_refs, o_refs, acts):
            o_ref[rows, :] = act(jnp.dot(h, w_ref[...], preferred_element_type=F32)).astype(o_ref.dtype)

    first = pl.program_id(1) == 0

    @pl.when(first)
    def _():
        for rows in _row_chunks(x_ref.shape[0]):
            h = _rms(x_ref[rows, :], g_ref[...]).astype(BF16)
            h_ref[rows, :] = h
            if extra_act is not None:
                z = jnp.dot(h, we_ref[...], preferred_element_type=F32) + be_ref[...]
                e_ref[rows, :] = extra_act(z)[:, :e_ref.shape[1]]
            project(h, rows)

    @pl.when(jnp.logical_not(first))
    def _():
        project(h_ref[...], slice(None))


def _norm_proj(x, g, w, layer, ns, acts, dtypes, *, name, extra=None, tm_pref=TILES["proj_rows"], tn_pref=TILES["proj_cols"]):
    m, d = x.shape
    n_sec = len(acts)
    tm = _pick_tile(m, tm_pref, SUBLANES)
    tn = _pick_tile(ns, tn_pref, LANES)
    tiles = ns // tn
    in_specs = [pl.BlockSpec((tm, d), lambda i, j: (i, 0)),
                pl.BlockSpec((1, d), lambda i, j: (0, 0))]
    in_specs += [pl.BlockSpec((None, d, tn), lambda i, j, s=s: (layer, 0, s * tiles + j)) for s in range(n_sec)]
    args = [x, g.reshape(1, d)] + [w] * n_sec
    out_specs = [pl.BlockSpec((tm, tn), lambda i, j: (i, j)) for _ in range(n_sec)]
    out_shape = [jax.ShapeDtypeStruct((m, ns), dt) for dt in dtypes]
    extra_act = None
    if extra is not None:
        w_e, b_e, n_e, extra_act = extra
        in_specs += [pl.BlockSpec(w_e.shape, lambda i, j: (0, 0)), pl.BlockSpec(b_e.shape, lambda i, j: (0, 0))]
        args += [w_e, b_e]
        out_specs.append(pl.BlockSpec((tm, n_e), lambda i, j: (i, 0)))
        out_shape.append(jax.ShapeDtypeStruct((m, n_e), F32))
    est = 2 * tm * d * 4 + tm * d * 2 + n_sec * 2 * (d * tn * 2 + tm * tn * 4) + 4 * tm * tn * 4
    return pl.pallas_call(
        functools.partial(_norm_proj_kernel, acts=tuple(acts), extra_act=extra_act),
        grid=(m // tm, tiles),
        in_specs=in_specs, out_specs=out_specs, out_shape=out_shape,
        scratch_shapes=[pltpu.VMEM((tm, d), BF16)],
        compiler_params=_params(("parallel", "arbitrary"), est),
        name=name,
    )(*args)


def _cumsum_kernel(x_ref, o_ref, carry_ref):
    @pl.when(pl.program_id(0) == 0)
    def _():
        carry_ref[...] = jnp.zeros_like(carry_ref)

    x = x_ref[...]
    n = x.shape[1]
    upper = (lax.broadcasted_iota(jnp.int32, (n, n), 0) <= lax.broadcasted_iota(jnp.int32, (n, n), 1)).astype(BF16)
    hi = x.astype(BF16)
    rest = x - hi.astype(F32)
    mid = rest.astype(BF16)
    lo = (rest - mid.astype(F32)).astype(BF16)
    local = (jnp.dot(hi, upper, preferred_element_type=F32) + jnp.dot(mid, upper, preferred_element_type=F32)
             + jnp.dot(lo, upper, preferred_element_type=F32))
    out = local + carry_ref[...]
    o_ref[...] = out
    carry_ref[...] = jnp.broadcast_to(out[:, n - 1:n], out.shape)


def _cumsum_lanes(x, *, name):
    r, l = x.shape
    return pl.pallas_call(
        _cumsum_kernel,
        grid=(l // LANES,),
        in_specs=[pl.BlockSpec((r, LANES), lambda t: (0, t))],
        out_specs=pl.BlockSpec((r, LANES), lambda t: (0, t)),
        out_shape=jax.ShapeDtypeStruct((r, l), F32),
        scratch_shapes=[pltpu.VMEM((r, LANES), F32)],
        compiler_params=_params(("arbitrary",), 8 * r * LANES * 4),
        name=name,
    )(x)


def _cumsum_rows_kernel(x_ref, o_ref, carry_ref):
    @pl.when(pl.program_id(1) == 0)
    def _():
        carry_ref[...] = jnp.zeros_like(carry_ref)

    x = x_ref[0]
    n = x.shape[0]
    lower = (lax.broadcasted_iota(jnp.int32, (n, n), 0) >= lax.broadcasted_iota(jnp.int32, (n, n), 1)).astype(BF16)
    hi = x.astype(BF16)
    rest = x - hi.astype(F32)
    mid = rest.astype(BF16)
    lo = (rest - mid.astype(F32)).astype(BF16)
    local = (jnp.dot(lower, hi, preferred_element_type=F32) + jnp.dot(lower, mid, preferred_element_type=F32)
             + jnp.dot(lower, lo, preferred_element_type=F32))
    out = local + carry_ref[...]
    o_ref[0] = out
    carry_ref[...] = out[n - 1:n, :]


def _cumsum_rows(x, *, name, tile_pref=TILES["cumsum_rows"]):
    b, s_len, w = x.shape
    tile = _pick_tile(s_len, tile_pref, SUBLANES)
    spec = pl.BlockSpec((1, tile, w), lambda bi, ti: (bi, ti, 0))
    return pl.pallas_call(
        _cumsum_rows_kernel,
        grid=(b, s_len // tile),
        in_specs=[spec], out_specs=spec,
        out_shape=jax.ShapeDtypeStruct((b, s_len, w), F32),
        scratch_shapes=[pltpu.VMEM((1, w), F32)],
        compiler_params=_params(("parallel", "arbitrary"), 8 * tile * w * 4 + 4 * tile * tile),
        name=name,
    )(x)


def _fox_attn_kernel(q_ref, k_ref, v_ref, c_ref, o_ref, kaug_ref, vt_ref, ta_ref, tb_ref, pa_ref, pb_ref,
                     acc_ref, m_ref, qt_ref, *, tq, heads):
    s_len, dh = q_ref.shape[1], q_ref.shape[2]
    nk = s_len // tq
    h = pl.program_id(1)

    @pl.when(h == 0)
    def _():
        head_lane = lax.broadcasted_iota(jnp.int32, c_ref.shape[1:], 1) < heads
        ckp = jnp.where(head_lane, c_ref[0] * (-LOG2E), 0.0)
        hi = ckp.astype(BF16).astype(F32)
        rest = ckp - hi
        mid = rest.astype(BF16).astype(F32)
        lo = (rest - mid).astype(BF16).astype(F32)
        aug = hi + pltpu.roll(mid, heads, axis=1) + pltpu.roll(lo, 2 * heads, axis=1)
        kaug_ref[:, dh:] = aug.astype(BF16)

    kaug_ref[:, :dh] = k_ref[0].astype(BF16)
    for kj in range(nk):
        vt_ref[kj, :dh] = v_ref[0, kj * tq:(kj + 1) * tq, :].T.astype(BF16)
        vt_ref[kj, dh:] = (lax.broadcasted_iota(jnp.int32, (ONES_ROWS, tq), 0) == 0).astype(BF16)

    row = lax.broadcasted_iota(jnp.int32, (LANES, tq), 0)
    pick_t = ((row == h) | (row == heads + h) | (row == 2 * heads + h)).astype(BF16)
    for qi in range(nk):
        qt_ref[qi, :dh] = q_ref[0, qi * tq:(qi + 1) * tq, :].T
        qt_ref[qi, dh:] = pick_t
    t_refs, p_refs = (ta_ref, tb_ref), (pa_ref, pb_ref)

    m_ref[...] = jnp.full(m_ref.shape, -jnp.inf, F32)
    acc_ref[...] = jnp.zeros(acc_ref.shape, F32)

    def scores(pair, masked, t_ref):
        qa, ka = pair
        t = jnp.dot(kaug_ref[ka * tq:(ka + 1) * tq, :], qt_ref[qa], preferred_element_type=F32)
        if masked:
            causal = lax.broadcasted_iota(jnp.int32, (tq, tq), 0) <= lax.broadcasted_iota(jnp.int32, (tq, tq), 1)
            t = jnp.where(causal, t, -jnp.inf)
        t_ref[...] = t
        return jnp.max(t, axis=0, keepdims=True)

    def softmax_update(pair, t_ref, p_ref, m_blk):
        qa, _ = pair
        m_prev = m_ref[qa]
        m_new = jnp.maximum(m_prev, m_blk)
        alpha = jnp.exp2(m_prev - m_new)
        p_ref[...] = jnp.exp2(t_ref[...] - m_new).astype(BF16)
        m_ref[qa] = m_new
        return alpha

    def accumulate(pair, p_ref, alpha):
        qa, ka = pair
        acc_ref[qa] = alpha * acc_ref[qa] + jnp.dot(vt_ref[ka], p_ref[...], preferred_element_type=F32)

    pairs = [(qa, qa) for qa in range(nk)] + [(qa, ka) for qa in range(nk) for ka in range(qa)]
    m_blk = alpha = None
    for e in range(len(pairs) + 2):
        if e >= 2:
            accumulate(pairs[e - 2], p_refs[e % 2], alpha)
        if 1 <= e <= len(pairs):
            alpha = softmax_update(pairs[e - 1], t_refs[(e - 1) % 2], p_refs[(e - 1) % 2], m_blk)
        if e < len(pairs):
            m_blk = scores(pairs[e], e < nk, t_refs[e % 2])

    for qi in range(nk):
        out_t = acc_ref[qi, :dh] / acc_ref[qi, dh:dh + 1]
        o_ref[0, qi * tq:(qi + 1) * tq, :] = out_t.T.astype(o_ref.dtype)


def _fox_attn(q, k, v, c, heads, *, name, tq_pref=TILES["attn_block"]):
    b, s_len, d = q.shape
    dh = d // heads
    assert 3 * heads <= LANES
    tq = _pick_tile(s_len, tq_pref, LANES)
    nk = s_len // tq
    qkv_spec = pl.BlockSpec((1, s_len, dh), lambda bi, hi: (bi, 0, hi))
    est = (2 * s_len * dh * (2 + 4 + 4 + 2) + 2 * s_len * LANES * 4 + s_len * (2 * dh + LANES) * 2
           + s_len * (dh + ONES_ROWS + 1) * 4 + s_len * (dh + LANES) * 2 + 3 * tq * tq * 4 + 8 * tq * tq * 4)
    return pl.pallas_call(
        functools.partial(_fox_attn_kernel, tq=tq, heads=heads),
        grid=(b, heads),
        in_specs=[qkv_spec, qkv_spec, qkv_spec,
                  pl.BlockSpec((1, s_len, LANES), lambda bi, hi: (bi, 0, 0))],
        out_specs=qkv_spec,
        out_shape=jax.ShapeDtypeStruct((b, s_len, d), BF16),
        scratch_shapes=[pltpu.VMEM((s_len, dh + LANES), BF16), pltpu.VMEM((nk, dh + ONES_ROWS, tq), BF16),
                        pltpu.VMEM((tq, tq), F32), pltpu.VMEM((tq, tq), F32),
                        pltpu.VMEM((tq, tq), BF16), pltpu.VMEM((tq, tq), BF16),
                        pltpu.VMEM((nk, dh + ONES_ROWS, tq), F32), pltpu.VMEM((nk, 1, tq), F32),
                        pltpu.VMEM((nk, dh + LANES, tq), BF16)],
        compiler_params=_params(("parallel", "arbitrary"), est),
        name=name,
    )(q, k, v, c)


def _fox_dec_kernel(q_ref, kn_ref, vn_ref, kc_ref, vc_ref, ckc_ref, ckn_ref, o_ref, m_scr, l_scr, acc_scr, mask_scr,
                    *, heads):
    step = pl.program_id(1)
    q = q_ref[0]
    rows, dh = q.shape
    tp = kc_ref.shape[1]
    shift = heads.bit_length() - 1

    def head_mask(keys, causal):
        r = lax.broadcasted_iota(jnp.int32, (rows, keys), 0)
        j = lax.broadcasted_iota(jnp.int32, (rows, keys), 1)
        visible = ((r ^ j) & (heads - 1)) == 0
        if causal:
            visible = visible & ((j >> shift) <= (r >> shift))
        return jnp.where(visible, 0.0, -jnp.inf)

    @pl.when(step == 0)
    def _():
        m_scr[...] = jnp.full(m_scr.shape, -jnp.inf, F32)
        l_scr[...] = jnp.zeros(l_scr.shape, F32)
        acc_scr[...] = jnp.zeros(acc_scr.shape, F32)
        mask_scr[...] = head_mask(tp * heads, False)

    def absorb(k_flat, v_flat, bias):
        t = lax.dot_general(q, k_flat.astype(BF16), (((1,), (1,)), ((), ())), preferred_element_type=F32) + bias
        m_prev = m_scr[...]
        m_new = jnp.maximum(m_prev, jnp.max(t, axis=-1, keepdims=True))
        p = jnp.exp2(t - m_new)
        alpha = jnp.exp2(m_prev - m_new)
        l_scr[...] = alpha * l_scr[...] + jnp.sum(p, axis=-1, keepdims=True)
        acc_scr[...] = alpha * acc_scr[...] + jnp.dot(p.astype(BF16), v_flat.astype(BF16), preferred_element_type=F32)
        m_scr[...] = m_new

    absorb(kc_ref[0].reshape(tp * heads, dh), vc_ref[0].reshape(tp * heads, dh), mask_scr[...] - ckc_ref[0] * LOG2E)

    @pl.when(step == pl.num_programs(1) - 1)
    def _():
        absorb(kn_ref[0], vn_ref[0], head_mask(rows, True) - ckn_ref[0] * LOG2E)
        o_ref[0] = (acc_scr[...] / l_scr[...]).astype(o_ref.dtype)


def _fox_dec_attn(q, k_new, v_new, k_cache, v_cache, layer, c_cache, c_new, *, name, tp_pref=TILES["cache_rows"]):
    b, rows, dh = q.shape
    _, _, p, heads, _ = k_cache.shape
    assert heads & (heads - 1) == 0
    tp = _pick_tile(p, tp_pref, SUBLANES)
    new_spec = pl.BlockSpec((1, rows, dh), lambda bi, pi: (bi, 0, 0))
    cache_spec = pl.BlockSpec((None, 1, tp, heads, dh), lambda bi, pi: (layer, bi, pi, 0, 0))
    est = 2 * 2 * tp * heads * dh * 4 + 2 * tp * heads * dh * 2 + 7 * rows * tp * heads * 4
    return pl.pallas_call(
        functools.partial(_fox_dec_kernel, heads=heads),
        grid=(b, p // tp),
        in_specs=[new_spec, new_spec, new_spec, cache_spec, cache_spec,
                  pl.BlockSpec((1, 1, tp * heads), lambda bi, pi: (bi, 0, pi)),
                  pl.BlockSpec((1, 1, rows), lambda bi, pi: (bi, 0, 0))],
        out_specs=new_spec,
        out_shape=jax.ShapeDtypeStruct((b, rows, dh), BF16),
        scratch_shapes=[pltpu.VMEM((rows, 1), F32), pltpu.VMEM((rows, 1), F32), pltpu.VMEM((rows, dh), F32),
                        pltpu.VMEM((rows, tp * heads), F32)],
        compiler_params=_params(("parallel", "arbitrary"), est),
        name=name,
    )(q, k_new, v_new, k_cache, v_cache, c_cache, c_new)


def _lru_kernel(u_ref, gate_ref, cw_ref, cb_ref, wa_ref, ba_ref, wx_ref, bx_ref, lam_ref, h0_ref, buf0_ref,
                y_ref, hl_ref, ul_ref, h_scr, tail_scr):
    @pl.when(pl.program_id(2) == 0)
    def _():
        h_scr[...] = h0_ref[0]
        tail_scr[...] = buf0_ref[0]

    u = u_ref[0]
    tc, bw = u.shape
    taps = cw_ref.shape[0]
    tail = tail_scr[...]
    head_rows = lax.broadcasted_iota(jnp.int32, (SUBLANES, bw), 0)

    def delayed(d):
        if d == 0:
            return u
        ru = pltpu.roll(u, d, axis=0)
        head = jnp.where(head_rows < d, pltpu.roll(tail, d, axis=0), ru[:SUBLANES])
        return head if tc == SUBLANES else jnp.concatenate([head, ru[SUBLANES:]], axis=0)

    conv = delayed(taps - 1) * cw_ref[0:1, :]
    for k in range(1, taps):
        conv = conv + delayed(taps - 1 - k) * cw_ref[k:k + 1, :]
    uc = cb_ref[...] + conv
    tail_scr[...] = u[tc - SUBLANES:, :]
    ul_ref[0] = u[tc - SUBLANES:, :]

    ucb = uc.astype(BF16)
    r = jax.nn.sigmoid(jnp.dot(ucb, wa_ref[0], preferred_element_type=F32) + ba_ref[...])
    ig = jax.nn.sigmoid(jnp.dot(ucb, wx_ref[0], preferred_element_type=F32) + bx_ref[...])
    log_a = (-LRU_C * _softplus(-lam_ref[...])) * r
    a = jnp.exp(log_a)
    th = jnp.tanh(log_a)
    decay = (-2.0 * th) / (1.0 - th)
    b = (decay * lax.rsqrt(jnp.maximum(decay, TINY))) * (ig * uc)

    groups = tc // SUBLANES
    a3 = a.reshape(groups, SUBLANES, bw)
    b3 = b.reshape(groups, SUBLANES, bw)
    sub = lax.broadcasted_iota(jnp.int32, (groups, SUBLANES, bw), 1)
    d = 1
    while d < SUBLANES:
        valid = sub >= d
        b3 = jnp.where(valid, a3 * pltpu.roll(b3, d, axis=1) + b3, b3)
        a3 = jnp.where(valid, a3 * pltpu.roll(a3, d, axis=1), a3)
        d *= 2
    h = h_scr[...]
    hs = []
    for gi in range(groups):
        hg = a3[gi] * h + b3[gi]
        hs.append(hg)
        h = hg[SUBLANES - 1:SUBLANES, :]
    h_scr[...] = h
    hl_ref[0] = hs[-1]
    hs = hs[0] if groups == 1 else jnp.concatenate(hs, axis=0)
    y_ref[0] = (hs * gate_ref[0].astype(F32)).astype(y_ref.dtype)


def _lru_scan(u, gate, conv_w, conv_b, w_ga, b_a, w_gx, b_x, lam, layer, h0, buf0, *, name, tc_pref=TILES["scan_rows"]):
    b, t, w = u.shape
    _, nb, bw, _ = w_ga.shape
    taps = conv_w.shape[0]
    tc = _pick_tile(t, tc_pref, SUBLANES)
    seq_spec = pl.BlockSpec((1, tc, bw), lambda bi, ni, ti: (bi, ti, ni))
    row_spec = pl.BlockSpec((1, bw), lambda bi, ni, ti: (0, ni))
    blk_spec = pl.BlockSpec((None, 1, bw, bw), lambda bi, ni, ti: (layer, ni, 0, 0))
    last_spec = pl.BlockSpec((1, SUBLANES, bw), lambda bi, ni, ti: (bi, 0, ni))
    est = 2 * tc * bw * (4 + 2 + 2) + 4 * bw * bw * 2 + 24 * tc * bw * 4
    return pl.pallas_call(
        _lru_kernel,
        grid=(b, nb, t // tc),
        in_specs=[seq_spec, seq_spec,
                  pl.BlockSpec((taps, bw), lambda bi, ni, ti: (0, ni)), row_spec,
                  blk_spec, row_spec, blk_spec, row_spec, row_spec,
                  pl.BlockSpec((1, 1, bw), lambda bi, ni, ti: (bi, 0, ni)), last_spec],
        out_specs=[seq_spec, last_spec, last_spec],
        out_shape=[jax.ShapeDtypeStruct((b, t, w), BF16),
                   jax.ShapeDtypeStruct((b, SUBLANES, w), F32),
                   jax.ShapeDtypeStruct((b, SUBLANES, w), F32)],
        scratch_shapes=[pltpu.VMEM((1, bw), F32), pltpu.VMEM((SUBLANES, bw), F32)],
        compiler_params=_params(("parallel", "parallel", "arbitrary"), est),
        name=name,
    )(u, gate, conv_w, conv_b.reshape(1, w), w_ga, b_a.reshape(1, w), w_gx, b_x.reshape(1, w),
      lam.reshape(1, w), h0, buf0)


def _xattn_kernel(x_ref, a_ref, wa_ref, g_ref, wq_ref, mk_ref, mv_ref, wo_ref, o_ref, *, heads):
    x = x_ref[0] + jnp.dot(a_ref[0], wa_ref[...], preferred_element_type=F32)
    h = _rms(x, g_ref[...]).astype(BF16)
    q = jnp.dot(h, wq_ref[...], preferred_element_type=F32).astype(BF16)
    dh = q.shape[1] // heads
    scale = dh ** -0.5
    outs = []
    for hd in range(heads):
        cols = slice(hd * dh, (hd + 1) * dh)
        s = lax.dot_general(q[:, cols], mk_ref[0, :, cols].astype(BF16), (((1,), (1,)), ((), ())),
                            preferred_element_type=F32) * scale
        p = jnp.exp(s - jnp.max(s, axis=-1, keepdims=True))
        l = jnp.sum(p, axis=-1, keepdims=True)
        o = jnp.dot(p.astype(BF16), mv_ref[0, :, cols].astype(BF16), preferred_element_type=F32)
        outs.append((o / l).astype(BF16))
    o_all = jnp.concatenate(outs, axis=-1)
    o_ref[0] = x + jnp.dot(o_all, wo_ref[...], preferred_element_type=F32)


def _xattn(x, a, wa, layer_a, g, wq, mk, mv, wo, layer, heads, *, name, tm_pref=TILES["xattn_rows"]):
    b, t, d = x.shape
    k = a.shape[2]
    nm, mw = mk.shape[1], mk.shape[2]
    tm = _pick_tile(t, tm_pref, SUBLANES)
    x_spec = pl.BlockSpec((1, tm, d), lambda bi, ti: (bi, ti, 0))
    mem_spec = pl.BlockSpec((1, nm, mw), lambda bi, ti: (bi, 0, 0))
    once = pl.Buffered(1)
    est = 4 * tm * d * 4 + 2 * tm * k * 2 + (k * d + 2 * d * mw) * 2 + 4 * nm * mw * 4 + 3 * tm * d * 4
    return pl.pallas_call(
        functools.partial(_xattn_kernel, heads=heads),
        grid=(b, t // tm),
        in_specs=[x_spec, pl.BlockSpec((1, tm, k), lambda bi, ti: (bi, ti, 0)),
                  pl.BlockSpec((None, k, d), lambda bi, ti: (layer_a, 0, 0), pipeline_mode=once),
                  pl.BlockSpec((1, d), lambda bi, ti: (0, 0)),
                  pl.BlockSpec((None, d, mw), lambda bi, ti: (layer, 0, 0), pipeline_mode=once), mem_spec, mem_spec,
                  pl.BlockSpec((None, mw, d), lambda bi, ti: (layer, 0, 0), pipeline_mode=once)],
        out_specs=x_spec,
        out_shape=jax.ShapeDtypeStruct((b, t, d), F32),
        compiler_params=_params(("parallel", "parallel"), est),
        name=name,
    )(x, a, wa, g.reshape(1, d), wq, mk, mv, wo)


def _ffn_kernel(x_ref, g_ref, wg_ref, wu_ref, wo_ref, gf_ref, o_ref, h_ref, *, final_norm):
    j = pl.program_id(1)

    def mix(h):
        gate = jnp.dot(h, wg_ref[...], preferred_element_type=F32)
        up = jnp.dot(h, wu_ref[...], preferred_element_type=F32)
        act = (jax.nn.silu(gate) * up).astype(BF16)
        return jnp.dot(act, wo_ref[...], preferred_element_type=F32)

    @pl.when(j == 0)
    def _():
        for rows in _row_chunks(x_ref.shape[0]):
            x = x_ref[rows, :]
            h = _rms(x, g_ref[...]).astype(BF16)
            h_ref[rows, :] = h
            o_ref[rows, :] = x + mix(h)

    @pl.when(j > 0)
    def _():
        o_ref[...] += mix(h_ref[...])

    if final_norm:
        @pl.when(j == pl.num_programs(1) - 1)
        def _():
            o_ref[...] = _rms(o_ref[...], gf_ref[...])


def _ffn(x, g, w_in, w_out, layer, g_final, *, final_norm, name, tm_pref=TILES["ffn_rows"], tf_pref=TILES["ffn_cols"]):
    m, d = x.shape
    f = w_out.shape[1]
    tm = _pick_tile(m, tm_pref, SUBLANES)
    tf = _pick_tile(f, tf_pref, LANES)
    nf = f // tf
    x_spec = pl.BlockSpec((tm, d), lambda i, j: (i, 0))
    vec_spec = pl.BlockSpec((1, d), lambda i, j: (0, 0))
    est = 4 * tm * d * 4 + tm * d * 2 + 2 * 3 * d * tf * 2 + 4 * tm * tf * 4
    return pl.pallas_call(
        functools.partial(_ffn_kernel, final_norm=final_norm),
        grid=(m // tm, nf),
        in_specs=[x_spec, vec_spec,
                  pl.BlockSpec((None, d, tf), lambda i, j: (layer, 0, j)),
                  pl.BlockSpec((None, d, tf), lambda i, j: (layer, 0, nf + j)),
                  pl.BlockSpec((None, tf, d), lambda i, j: (layer, j, 0)), vec_spec],
        out_specs=x_spec,
        out_shape=jax.ShapeDtypeStruct((m, d), F32),
        scratch_shapes=[pltpu.VMEM((tm, d), BF16)],
        compiler_params=_params(("parallel", "arbitrary"), est),
        name=name,
    )(x, g.reshape(1, d), w_in, w_in, w_out, g_final.reshape(1, d))


def kernel(x_prompt, x_sample, mem_prompt, cache_fox_k, cache_fox_v, cache_fox_logf, cache_mem_k, cache_mem_v, state_lru_h, state_lru_conv, norm_mix, norm_mem, norm_xattn, norm_ffn, norm_final, fox_w_in, fox_b_f, fox_w_out, lru_w_in, lru_conv_w, lru_conv_b, lru_w_ga, lru_b_a, lru_w_gx, lru_b_x, lru_lambda, lru_w_out, xattn_w_q, xattn_w_kv, xattn_w_o, ffn_w_in, ffn_w_out):
    b, s_len, d = x_prompt.shape
    bd, t_dec, _ = x_sample.shape
    depth = norm_mix.shape[0]
    heads = fox_b_f.shape[1]
    past = cache_fox_k.shape[2]
    n_mem, mem_heads = cache_mem_k.shape[2], cache_mem_k.shape[3]
    mem_w = mem_heads * cache_mem_k.shape[4]
    taps = lru_conv_w.shape[1]
    assert taps - 1 <= SUBLANES <= min(s_len, t_dec) and heads <= LANES

    xp = x_prompt.reshape(b * s_len, d)
    xs = x_sample.reshape(bd * t_dec, d)
    mem = mem_prompt.reshape(b * n_mem, d)
    pk, pv, plf, pmk, pmv, ph, pc = [], [], [], [], [], [], []
    sk, sv, slf, sh, sc = [], [], [], [], []

    fox_in, fox_out = fox_w_in.astype(BF16), fox_w_out.astype(BF16)
    lru_in, lru_out = lru_w_in.astype(BF16), lru_w_out.astype(BF16)
    lru_ga, lru_gx = lru_w_ga.astype(BF16), lru_w_gx.astype(BF16)
    xq, xkv, xo = xattn_w_q.astype(BF16), xattn_w_kv.astype(BF16), xattn_w_o.astype(BF16)
    ffn_in, ffn_out = ffn_w_in.astype(BF16), ffn_w_out.astype(BF16)

    for i in range(depth):
        j = i // 2
        if i % 2 == 0:
            w_f = jnp.pad(fox_in[j, :, 3 * d:], ((0, 0), (0, LANES - heads)))
            b_f = jnp.pad(fox_b_f[j], (0, LANES - heads)).reshape(1, LANES)
            q_scale = (d // heads) ** -0.5 * LOG2E
            proj = functools.partial(_norm_proj, g=norm_mix[i], w=fox_in, layer=j, ns=d, dtypes=[BF16, F32, F32],
                                     acts=[lambda q: q * q_scale, _identity, _identity],
                                     extra=(w_f, b_f, LANES, _log_sigmoid))

            qp, kp, vp, lfp = proj(xp, name=f"fox_proj_p{i}")
            c_p = _cumsum_rows(lfp.reshape(b, s_len, LANES), name=f"fox_cumsum_p{i}")
            attn = _fox_attn(qp.reshape(b, s_len, d), kp.reshape(b, s_len, d), vp.reshape(b, s_len, d), c_p, heads,
                             name=f"fox_attn_p{i}")
            mixed_p, mixed_s, w_mix = attn, None, fox_out
            pk.append(kp.reshape(b, s_len, heads, d // heads))
            pv.append(vp.reshape(b, s_len, heads, d // heads))
            plf.append(lfp[:, :heads].reshape(b, s_len, heads))

            qs, ks, vs, lfs = proj(xs, name=f"fox_proj_s{i}")
            lfs = lfs[:, :heads]
            lf_all = jnp.concatenate([cache_fox_logf[j], lfs.reshape(bd, t_dec, heads)], axis=1)
            total = past + t_dec
            padded = -(-total // LANES) * LANES
            lf_t = jnp.pad(lf_all.transpose(0, 2, 1).reshape(bd * heads, total), ((0, 0), (0, padded - total)))
            c_s = _cumsum_lanes(lf_t, name=f"fox_cumsum_s{i}")
            c_s = c_s.reshape(bd, heads, padded).transpose(0, 2, 1)
            rows = t_dec * heads
            attn = _fox_dec_attn(qs.reshape(bd, rows, d // heads), ks.reshape(bd, rows, d // heads),
                                 vs.reshape(bd, rows, d // heads), cache_fox_k, cache_fox_v, j,
                                 c_s[:, :past].reshape(bd, 1, past * heads), c_s[:, past:total].reshape(bd, 1, rows),
                                 name=f"fox_attn_s{i}")
            mixed_s = attn.reshape(bd, t_dec, d)
            sk.append(ks.reshape(bd, t_dec, heads, d // heads))
            sv.append(vs.reshape(bd, t_dec, heads, d // heads))
            slf.append(lfs.reshape(bd, t_dec, heads))
        else:
            width = lru_in.shape[2] // 2
            proj = functools.partial(_norm_proj, g=norm_mix[i], w=lru_in, layer=j, ns=width,
                                     acts=[jax.nn.gelu, _identity], dtypes=[BF16, F32])
            scan = functools.partial(_lru_scan, conv_w=lru_conv_w[j], conv_b=lru_conv_b[j], w_ga=lru_ga, b_a=lru_b_a[j],
                                     w_gx=lru_gx, b_x=lru_b_x[j], lam=lru_lambda[j], layer=j)

            gate, u = proj(xp, name=f"lru_proj_p{i}")
            y, h8, u8 = scan(u.reshape(b, s_len, width), gate.reshape(b, s_len, width),
                             h0=jnp.zeros((b, 1, width), F32), buf0=jnp.zeros((b, SUBLANES, width), F32),
                             name=f"lru_scan_p{i}")
            mixed_p, w_mix = y, lru_out
            ph.append(h8[:, SUBLANES - 1])
            pc.append(u8[:, SUBLANES - (taps - 1):])

            gate, u = proj(xs, name=f"lru_proj_s{i}")
            buf0 = jnp.pad(state_lru_conv[j], ((0, 0), (SUBLANES - (taps - 1), 0), (0, 0)))
            y, h8, u8 = scan(u.reshape(bd, t_dec, width), gate.reshape(bd, t_dec, width),
                             h0=state_lru_h[j].reshape(bd, 1, width), buf0=buf0, name=f"lru_scan_s{i}")
            mixed_s = y
            sh.append(h8[:, SUBLANES - 1])
            sc.append(u8[:, SUBLANES - (taps - 1):])

        mk, mv = _norm_proj(mem, norm_mem[i], xkv, i, mem_w, [_identity] * 2, [F32, F32], name=f"mem_kv{i}")
        pmk.append(mk.reshape(b, n_mem, mem_heads, mem_w // mem_heads))
        pmv.append(mv.reshape(b, n_mem, mem_heads, mem_w // mem_heads))
        xp = _xattn(xp.reshape(b, s_len, d), mixed_p, w_mix, j, norm_xattn[i], xq, mk.reshape(b, n_mem, mem_w),
                    mv.reshape(b, n_mem, mem_w), xo, i, mem_heads, name=f"xattn_p{i}").reshape(b * s_len, d)
        xs = _xattn(xs.reshape(bd, t_dec, d), mixed_s, w_mix, j, norm_xattn[i], xq, cache_mem_k[i].reshape(bd, n_mem, mem_w),
                    cache_mem_v[i].reshape(bd, n_mem, mem_w), xo, i, mem_heads, name=f"xattn_s{i}").reshape(bd * t_dec, d)

        last = i == depth - 1
        xp = _ffn(xp, norm_ffn[i], ffn_in, ffn_out, i, norm_final, final_norm=last, name=f"ffn_p{i}")
        xs = _ffn(xs, norm_ffn[i], ffn_in, ffn_out, i, norm_final, final_norm=last, name=f"ffn_s{i}")

    return (xp.reshape(b, s_len, d), xs.reshape(bd, t_dec, d),
            jnp.stack(pk), jnp.stack(pv), jnp.stack(plf), jnp.stack(pmk), jnp.stack(pmv),
            jnp.stack(ph), jnp.stack(pc),
            jnp.stack(sk), jnp.stack(sv), jnp.stack(slf), jnp.stack(sh), jnp.stack(sc))
```

```python
import functools

import jax
import jax.numpy as jnp
from jax import lax
from jax.experimental import pallas as pl
from jax.experimental.pallas import tpu as pltpu

F32 = jnp.float32
BF16 = jnp.bfloat16

EPS = 1e-6
LOG2E = 1.4426950408889634
TINY = 1e-30
LRU_C = 8.0
LANES = 128
SUBLANES = 8
ONES_ROWS = 16
NORM_CHUNKS = 4
V7X_VMEM_BUDGET = 56 * 1024 * 1024
VMEM_FLOOR = 16 * 1024 * 1024
VMEM_SLACK = 2

TILES = dict(proj_rows=1024, proj_cols=512, lru_proj_cols=1024, cumsum_rows=512, attn_block=512, attn_heads=2,
             cache_batch=2, cache_rows=256, scan_rows=4096, xattn_rows=512, ffn_rows=1024, ffn_cols=512)


def _pick_tile(n, pref, align):
    if n <= pref:
        return n
    t = (pref // align) * align
    while t >= align:
        if n % t == 0:
            return t
        t -= align
    raise ValueError(f"no {align}-aligned tile of {n} below {pref}")


def _params(semantics, vmem_estimate):
    limit = int(min(V7X_VMEM_BUDGET, max(VMEM_SLACK * vmem_estimate, VMEM_FLOOR)))
    return pltpu.CompilerParams(dimension_semantics=semantics, vmem_limit_bytes=limit)


def _rms(xf, g):
    return xf * lax.rsqrt(jnp.mean(xf * xf, axis=-1, keepdims=True) + EPS) * g


def _identity(x):
    return x


def _row_chunks(rows):
    step = rows // NORM_CHUNKS if rows % (NORM_CHUNKS * 2 * SUBLANES) == 0 else rows
    return [slice(r0, r0 + step) for r0 in range(0, rows, step)]


def _softplus(x):
    return jnp.maximum(x, 0.0) + jnp.log1p(jnp.exp(-jnp.abs(x)))


def _log_sigmoid(x):
    return -_softplus(-x)


def _sigmoid(x):
    return 0.5 * jnp.tanh(0.5 * x) + 0.5


def _norm_proj_kernel(*refs, acts, extra_act):
    n_sec = len(acts)
    x_ref, g_ref = refs[0], refs[1]
    w_refs = refs[2:2 + n_sec]
    pos = 2 + n_sec
    if extra_act is not None:
        we_ref, be_ref = refs[pos], refs[pos + 1]
        pos += 2
    o_refs = refs[pos:pos + n_sec]
    pos += n_sec
    if extra_act is not None:
        e_ref = refs[pos]
        pos += 1
    h_ref = refs[pos]

    def project(h, rows):
        for w_ref, o_ref, act in zip(w_refs, o_refs, acts):
            o_ref[rows, :] = act(jnp.dot(h, w_ref[...], preferred_element_type=F32)).astype(o_ref.dtype)

    first = pl.program_id(1) == 0

    @pl.when(first)
    def _():
        for rows in _row_chunks(x_ref.shape[0]):
            h = _rms(x_ref[rows, :], g_ref[...]).astype(BF16)
            h_ref[rows, :] = h
            if extra_act is not None:
                z = jnp.dot(h, we_ref[...], preferred_element_type=F32) + be_ref[...]
                e_ref[rows, :] = extra_act(z)[:, :e_ref.shape[1]]
            project(h, rows)

    @pl.when(jnp.logical_not(first))
    def _():
        project(h_ref[...], slice(None))


def _norm_proj(x, g, w, layer, ns, acts, dtypes, *, name, extra=None, tm_pref=TILES["proj_rows"], tn_pref=TILES["proj_cols"]):
    m, d = x.shape
    n_sec = len(acts)
    tm = _pick_tile(m, tm_pref, SUBLANES)
    tn = _pick_tile(ns, tn_pref, LANES)
    tiles = ns // tn
    in_specs = [pl.BlockSpec((tm, d), lambda i, j: (i, 0)),
                pl.BlockSpec((1, d), lambda i, j: (0, 0))]
    in_specs += [pl.BlockSpec((None, d, tn), lambda i, j, s=s: (layer, 0, s * tiles + j)) for s in range(n_sec)]
    args = [x, g.reshape(1, d)] + [w] * n_sec
    out_specs = [pl.BlockSpec((tm, tn), lambda i, j: (i, j)) for _ in range(n_sec)]
    out_shape = [jax.ShapeDtypeStruct((m, ns), dt) for dt in dtypes]
    extra_act = None
    if extra is not None:
        w_e, b_e, n_e, extra_act = extra
        in_specs += [pl.BlockSpec(w_e.shape, lambda i, j: (0, 0)), pl.BlockSpec(b_e.shape, lambda i, j: (0, 0))]
        args += [w_e, b_e]
        out_specs.append(pl.BlockSpec((tm, n_e), lambda i, j: (i, 0)))
        out_shape.append(jax.ShapeDtypeStruct((m, n_e), F32))
    est = 2 * tm * d * 4 + tm * d * 2 + n_sec * 2 * (d * tn * 2 + tm * tn * 4) + 4 * tm * tn * 4
    return pl.pallas_call(
        functools.partial(_norm_proj_kernel, acts=tuple(acts), extra_act=extra_act),
        grid=(m // tm, tiles),
        in_specs=in_specs, out_specs=out_specs, out_shape=out_shape,
        scratch_shapes=[pltpu.VMEM((tm, d), BF16)],
        compiler_params=_params(("parallel", "arbitrary"), est),
        name=name,
    )(*args)


def _cumsum_kernel(x_ref, o_ref, carry_ref):
    @pl.when(pl.program_id(0) == 0)
    def _():
        carry_ref[...] = jnp.zeros_like(carry_ref)

    x = x_ref[...]
    n = x.shape[1]
    upper = (lax.broadcasted_iota(jnp.int32, (n, n), 0) <= lax.broadcasted_iota(jnp.int32, (n, n), 1)).astype(BF16)
    hi = x.astype(BF16)
    rest = x - hi.astype(F32)
    mid = rest.astype(BF16)
    lo = (rest - mid.astype(F32)).astype(BF16)
    local = (jnp.dot(hi, upper, preferred_element_type=F32) + jnp.dot(mid, upper, preferred_element_type=F32)
             + jnp.dot(lo, upper, preferred_element_type=F32))
    out = local + carry_ref[...]
    o_ref[...] = out
    carry_ref[...] = jnp.broadcast_to(out[:, n - 1:n], out.shape)


def _cumsum_lanes(x, *, name):
    r, l = x.shape
    return pl.pallas_call(
        _cumsum_kernel,
        grid=(l // LANES,),
        in_specs=[pl.BlockSpec((r, LANES), lambda t: (0, t))],
        out_specs=pl.BlockSpec((r, LANES), lambda t: (0, t)),
        out_shape=jax.ShapeDtypeStruct((r, l), F32),
        scratch_shapes=[pltpu.VMEM((r, LANES), F32)],
        compiler_params=_params(("arbitrary",), 8 * r * LANES * 4),
        name=name,
    )(x)


def _cumsum_rows_kernel(x_ref, o_ref, carry_ref):
    @pl.when(pl.program_id(1) == 0)
    def _():
        carry_ref[...] = jnp.zeros_like(carry_ref)

    x = x_ref[0]
    n = x.shape[0]
    lower = (lax.broadcasted_iota(jnp.int32, (n, n), 0) >= lax.broadcasted_iota(jnp.int32, (n, n), 1)).astype(BF16)
    hi = x.astype(BF16)
    rest = x - hi.astype(F32)
    mid = rest.astype(BF16)
    lo = (rest - mid.astype(F32)).astype(BF16)
    local = (jnp.dot(lower, hi, preferred_element_type=F32) + jnp.dot(lower, mid, preferred_element_type=F32)
             + jnp.dot(lower, lo, preferred_element_type=F32))
    out = local + carry_ref[...]
    o_ref[0] = out
    carry_ref[...] = out[n - 1:n, :]


def _cumsum_rows(x, *, name, tile_pref=TILES["cumsum_rows"]):
    b, s_len, w = x.shape
    tile = _pick_tile(s_len, tile_pref, SUBLANES)
    spec = pl.BlockSpec((1, tile, w), lambda bi, ti: (bi, ti, 0))
    return pl.pallas_call(
        _cumsum_rows_kernel,
        grid=(b, s_len // tile),
        in_specs=[spec], out_specs=spec,
        out_shape=jax.ShapeDtypeStruct((b, s_len, w), F32),
        scratch_shapes=[pltpu.VMEM((1, w), F32)],
        compiler_params=_params(("parallel", "arbitrary"), 8 * tile * w * 4 + 4 * tile * tile),
        name=name,
    )(x)


def _fox_attn_kernel(q_ref, k_ref, v_ref, c_ref, o_ref, aug_ref, kb_ref, vt_ref, ta_ref, tb_ref, pa_ref, pb_ref,
                     acc_ref, m_ref, qt_ref, *, tq, heads, group):
    s_len = q_ref.shape[1]
    dh = q_ref.shape[2] // group
    nk = s_len // tq

    @pl.when(pl.program_id(1) == 0)
    def _():
        head_lane = lax.broadcasted_iota(jnp.int32, c_ref.shape[1:], 1) < heads
        ckp = jnp.where(head_lane, c_ref[0] * (-LOG2E), 0.0)
        hi = ckp.astype(BF16).astype(F32)
        rest = ckp - hi
        mid = rest.astype(BF16).astype(F32)
        lo = (rest - mid).astype(BF16).astype(F32)
        aug = hi + pltpu.roll(mid, heads, axis=1) + pltpu.roll(lo, 2 * heads, axis=1)
        aug_ref[...] = aug.astype(BF16)

    row = lax.broadcasted_iota(jnp.int32, (LANES, tq), 0)
    ones_rows = (lax.broadcasted_iota(jnp.int32, (ONES_ROWS, tq), 0) == 0).astype(BF16)
    for s in range(group):
        cols = slice(s * dh, (s + 1) * dh)
        h = pl.program_id(1) * group + s
        kb_ref[s] = k_ref[0, :, cols].astype(BF16)
        pick_t = ((row == h) | (row == heads + h) | (row == 2 * heads + h)).astype(BF16)
        for blk in range(nk):
            rows = slice(blk * tq, (blk + 1) * tq)
            vt_ref[s, blk, :dh] = v_ref[0, rows, cols].T.astype(BF16)
            vt_ref[s, blk, dh:] = ones_rows
            qt_ref[s, blk, :dh] = q_ref[0, rows, cols].T
            qt_ref[s, blk, dh:] = pick_t
    t_refs, p_refs = (ta_ref, tb_ref), (pa_ref, pb_ref)
    m_ref[...] = jnp.full(m_ref.shape, -jnp.inf, F32)
    acc_ref[...] = jnp.zeros(acc_ref.shape, F32)

    def scores(s, pair, masked, t_ref):
        qa, ka = pair
        rows = slice(ka * tq, (ka + 1) * tq)
        kaug = jnp.concatenate([kb_ref[s, rows, :], aug_ref[rows, :]], axis=1)
        t = jnp.dot(kaug, qt_ref[s, qa], preferred_element_type=F32)
        if masked:
            causal = lax.broadcasted_iota(jnp.int32, (tq, tq), 0) <= lax.broadcasted_iota(jnp.int32, (tq, tq), 1)
            t = jnp.where(causal, t, -jnp.inf)
        t_ref[s] = t
        return jnp.max(t, axis=0, keepdims=True)

    def softmax_update(s, pair, t_ref, p_ref, m_blk):
        qa, _ = pair
        m_prev = m_ref[s, qa]
        m_new = jnp.maximum(m_prev, m_blk)
        alpha = jnp.exp2(m_prev - m_new)
        p_ref[s] = jnp.exp2(t_ref[s] - m_new).astype(BF16)
        m_ref[s, qa] = m_new
        return alpha

    def accumulate(s, pair, p_ref, alpha):
        qa, ka = pair
        acc_ref[s, qa] = alpha * acc_ref[s, qa] + jnp.dot(vt_ref[s, ka], p_ref[s], preferred_element_type=F32)

    pairs = [(qa, qa) for qa in range(nk)] + [(qa, ka) for qa in range(nk) for ka in range(qa)]
    m_blk, alpha = [None] * group, [None] * group
    for e in range(len(pairs) + 2):
        for s in range(group):
            if e >= 2:
                accumulate(s, pairs[e - 2], p_refs[e % 2], alpha[s])
            if 1 <= e <= len(pairs):
                alpha[s] = softmax_update(s, pairs[e - 1], t_refs[(e - 1) % 2], p_refs[(e - 1) % 2], m_blk[s])
            if e < len(pairs):
                m_blk[s] = scores(s, pairs[e], e < nk, t_refs[e % 2])

    for s in range(group):
        for qi in range(nk):
            out_t = acc_ref[s, qi, :dh] / acc_ref[s, qi, dh:dh + 1]
            o_ref[0, qi * tq:(qi + 1) * tq, s * dh:(s + 1) * dh] = out_t.T.astype(o_ref.dtype)


def _fox_attn(q, k, v, c, heads, *, name, tq_pref=TILES["attn_block"], group=TILES["attn_heads"]):
    b, s_len, d = q.shape
    dh = d // heads
    group = group if heads % group == 0 else 1
    assert 3 * heads <= LANES
    tq = _pick_tile(s_len, tq_pref, LANES)
    nk = s_len // tq
    qkv_spec = pl.BlockSpec((1, s_len, group * dh), lambda bi, hi: (bi, 0, hi))
    est = (group * (2 * s_len * dh * (2 + 4 + 4 + 2) + s_len * dh * 2 + s_len * (dh + ONES_ROWS) * (2 + 4)
                    + s_len * (dh + LANES) * 2 + 6 * tq * tq * 2) + 3 * s_len * LANES * 4 + 8 * tq * tq * 4)
    return pl.pallas_call(
        functools.partial(_fox_attn_kernel, tq=tq, heads=heads, group=group),
        grid=(b, heads // group),
        in_specs=[qkv_spec, qkv_spec, qkv_spec,
                  pl.BlockSpec((1, s_len, LANES), lambda bi, hi: (bi, 0, 0))],
        out_specs=qkv_spec,
        out_shape=jax.ShapeDtypeStruct((b, s_len, d), BF16),
        scratch_shapes=[pltpu.VMEM((s_len, LANES), BF16), pltpu.VMEM((group, s_len, dh), BF16),
                        pltpu.VMEM((group, nk, dh + ONES_ROWS, tq), BF16),
                        pltpu.VMEM((group, tq, tq), F32), pltpu.VMEM((group, tq, tq), F32),
                        pltpu.VMEM((group, tq, tq), BF16), pltpu.VMEM((group, tq, tq), BF16),
                        pltpu.VMEM((group, nk, dh + ONES_ROWS, tq), F32), pltpu.VMEM((group, nk, 1, tq), F32),
                        pltpu.VMEM((group, nk, dh + LANES, tq), BF16)],
        compiler_params=_params(("parallel", "arbitrary"), est),
        name=name,
    )(q, k, v, c)


def _fox_dec_kernel(q_ref, kn_ref, vn_ref, kc_ref, vc_ref, ckc_ref, ckn_ref, o_ref, m_scr, l_scr, acc_scr, mask_scr,
                    *, heads):
    step = pl.program_id(1)
    group, rows, dh = q_ref.shape
    tp = kc_ref.shape[1]
    shift = heads.bit_length() - 1

    def head_mask(keys, causal):
        r = lax.broadcasted_iota(jnp.int32, (rows, keys), 0)
        j = lax.broadcasted_iota(jnp.int32, (rows, keys), 1)
        visible = ((r ^ j) & (heads - 1)) == 0
        if causal:
            visible = visible & ((j >> shift) <= (r >> shift))
        return jnp.where(visible, 0.0, -jnp.inf)

    @pl.when(step == 0)
    def _():
        m_scr[...] = jnp.full(m_scr.shape, -jnp.inf, F32)
        l_scr[...] = jnp.zeros(l_scr.shape, F32)
        acc_scr[...] = jnp.zeros(acc_scr.shape, F32)
        mask_scr[...] = head_mask(tp * heads, False)

    def absorb(g, k_flat, v_flat, bias):
        t = lax.dot_general(q_ref[g], k_flat.astype(BF16), (((1,), (1,)), ((), ())), preferred_element_type=F32) + bias
        m_prev = m_scr[g]
        m_new = jnp.maximum(m_prev, jnp.max(t, axis=-1, keepdims=True))
        p = jnp.exp2(t - m_new)
        alpha = jnp.exp2(m_prev - m_new)
        l_scr[g] = alpha * l_scr[g] + jnp.sum(p, axis=-1, keepdims=True)
        acc_scr[g] = alpha * acc_scr[g] + jnp.dot(p.astype(BF16), v_flat.astype(BF16), preferred_element_type=F32)
        m_scr[g] = m_new

    for g in range(group):
        absorb(g, kc_ref[g].reshape(tp * heads, dh), vc_ref[g].reshape(tp * heads, dh), mask_scr[...] - ckc_ref[g] * LOG2E)

    @pl.when(step == pl.num_programs(1) - 1)
    def _():
        for g in range(group):
            absorb(g, kn_ref[g], vn_ref[g], head_mask(rows, True) - ckn_ref[g] * LOG2E)
            o_ref[g] = (acc_scr[g] / l_scr[g]).astype(o_ref.dtype)


def _fox_dec_attn(q, k_new, v_new, k_cache, v_cache, layer, c_cache, c_new, *, name, tp_pref=TILES["cache_rows"],
                  group=TILES["cache_batch"]):
    b, rows, dh = q.shape
    _, _, p, heads, _ = k_cache.shape
    assert heads & (heads - 1) == 0
    group = group if b % group == 0 else 1
    tp = _pick_tile(p, tp_pref, SUBLANES)
    new_spec = pl.BlockSpec((group, rows, dh), lambda bi, pi: (bi, 0, 0))
    cache_spec = pl.BlockSpec((None, group, tp, heads, dh), lambda bi, pi: (layer, bi, pi, 0, 0))
    est = group * (2 * 2 * tp * heads * dh * 4 + 2 * tp * heads * dh * 2 + 6 * rows * tp * heads * 4) + rows * tp * heads * 4
    return pl.pallas_call(
        functools.partial(_fox_dec_kernel, heads=heads),
        grid=(b // group, p // tp),
        in_specs=[new_spec, new_spec, new_spec, cache_spec, cache_spec,
                  pl.BlockSpec((group, 1, tp * heads), lambda bi, pi: (bi, 0, pi)),
                  pl.BlockSpec((group, 1, rows), lambda bi, pi: (bi, 0, 0))],
        out_specs=new_spec,
        out_shape=jax.ShapeDtypeStruct((b, rows, dh), BF16),
        scratch_shapes=[pltpu.VMEM((group, rows, 1), F32), pltpu.VMEM((group, rows, 1), F32),
                        pltpu.VMEM((group, rows, dh), F32), pltpu.VMEM((rows, tp * heads), F32)],
        compiler_params=_params(("parallel", "arbitrary"), est),
        name=name,
    )(q, k_new, v_new, k_cache, v_cache, c_cache, c_new)


def _lru_kernel(u_ref, gate_ref, cw_ref, cb_ref, wa_ref, ba_ref, wx_ref, bx_ref, lam_ref, h0_ref, buf0_ref,
                y_ref, hl_ref, ul_ref, h_scr, tail_scr):
    @pl.when(pl.program_id(2) == 0)
    def _():
        h_scr[...] = h0_ref[0]
        tail_scr[...] = buf0_ref[0]

    u = u_ref[0]
    tc, bw = u.shape
    taps = cw_ref.shape[0]
    tail = tail_scr[...]
    head_rows = lax.broadcasted_iota(jnp.int32, (SUBLANES, bw), 0)

    def delayed(d):
        if d == 0:
            return u
        ru = pltpu.roll(u, d, axis=0)
        head = jnp.where(head_rows < d, pltpu.roll(tail, d, axis=0), ru[:SUBLANES])
        return head if tc == SUBLANES else jnp.concatenate([head, ru[SUBLANES:]], axis=0)

    conv = delayed(taps - 1) * cw_ref[0:1, :]
    for k in range(1, taps):
        conv = conv + delayed(taps - 1 - k) * cw_ref[k:k + 1, :]
    uc = cb_ref[...] + conv
    tail_scr[...] = u[tc - SUBLANES:, :]
    ul_ref[0] = u[tc - SUBLANES:, :]

    ucb = uc.astype(BF16)
    r = _sigmoid(jnp.dot(ucb, wa_ref[0], preferred_element_type=F32) + ba_ref[...])
    ig = _sigmoid(jnp.dot(ucb, wx_ref[0], preferred_element_type=F32) + bx_ref[...])
    log_a = (-LRU_C * _softplus(-lam_ref[...])) * r
    a = jnp.exp(log_a)
    th = jnp.tanh(log_a)
    decay = (-2.0 * th) / (1.0 - th)
    b = (decay * lax.rsqrt(jnp.maximum(decay, TINY))) * (ig * uc)

    groups = tc // SUBLANES
    a3 = a.reshape(groups, SUBLANES, bw)
    b3 = b.reshape(groups, SUBLANES, bw)
    sub = lax.broadcasted_iota(jnp.int32, (groups, SUBLANES, bw), 1)
    d = 1
    while d < SUBLANES:
        valid = sub >= d
        b3 = jnp.where(valid, a3 * pltpu.roll(b3, d, axis=1) + b3, b3)
        a3 = jnp.where(valid, a3 * pltpu.roll(a3, d, axis=1), a3)
        d *= 2
    h = h_scr[...]
    hs = []
    for gi in range(groups):
        hg = a3[gi] * h + b3[gi]
        hs.append(hg)
        h = hg[SUBLANES - 1:SUBLANES, :]
    h_scr[...] = h
    hl_ref[0] = hs[-1]
    hs = hs[0] if groups == 1 else jnp.concatenate(hs, axis=0)
    y_ref[0] = (hs * gate_ref[0].astype(F32)).astype(y_ref.dtype)


def _lru_scan(u, gate, conv_w, conv_b, w_ga, b_a, w_gx, b_x, lam, layer, h0, buf0, *, name, tc_pref=TILES["scan_rows"]):
    b, t, w = u.shape
    _, nb, bw, _ = w_ga.shape
    taps = conv_w.shape[0]
    tc = _pick_tile(t, tc_pref, SUBLANES)
    seq_spec = pl.BlockSpec((1, tc, bw), lambda bi, ni, ti: (bi, ti, ni))
    row_spec = pl.BlockSpec((1, bw), lambda bi, ni, ti: (0, ni))
    blk_spec = pl.BlockSpec((None, 1, bw, bw), lambda bi, ni, ti: (layer, ni, 0, 0))
    last_spec = pl.BlockSpec((1, SUBLANES, bw), lambda bi, ni, ti: (bi, 0, ni))
    est = 2 * tc * bw * (4 + 2 + 2) + 4 * bw * bw * 2 + 24 * tc * bw * 4
    return pl.pallas_call(
        _lru_kernel,
        grid=(b, nb, t // tc),
        in_specs=[seq_spec, seq_spec,
                  pl.BlockSpec((taps, bw), lambda bi, ni, ti: (0, ni)), row_spec,
                  blk_spec, row_spec, blk_spec, row_spec, row_spec,
                  pl.BlockSpec((1, 1, bw), lambda bi, ni, ti: (bi, 0, ni)), last_spec],
        out_specs=[seq_spec, last_spec, last_spec],
        out_shape=[jax.ShapeDtypeStruct((b, t, w), BF16),
                   jax.ShapeDtypeStruct((b, SUBLANES, w), F32),
                   jax.ShapeDtypeStruct((b, SUBLANES, w), F32)],
        scratch_shapes=[pltpu.VMEM((1, bw), F32), pltpu.VMEM((SUBLANES, bw), F32)],
        compiler_params=_params(("parallel", "parallel", "arbitrary"), est),
        name=name,
    )(u, gate, conv_w, conv_b.reshape(1, w), w_ga, b_a.reshape(1, w), w_gx, b_x.reshape(1, w),
      lam.reshape(1, w), h0, buf0)


def _xattn_kernel(x_ref, a_ref, wa_ref, g_ref, wq_ref, mk_ref, mv_ref, wo_ref, o_ref, *, heads):
    x = x_ref[0] + jnp.dot(a_ref[0], wa_ref[...], preferred_element_type=F32)
    h = _rms(x, g_ref[...]).astype(BF16)
    q = jnp.dot(h, wq_ref[...], preferred_element_type=F32).astype(BF16)
    dh = q.shape[1] // heads
    scale = dh ** -0.5
    outs = []
    for hd in range(heads):
        cols = slice(hd * dh, (hd + 1) * dh)
        s = lax.dot_general(q[:, cols], mk_ref[0, :, cols].astype(BF16), (((1,), (1,)), ((), ())),
                            preferred_element_type=F32) * scale
        p = jnp.exp(s - jnp.max(s, axis=-1, keepdims=True))
        l = jnp.sum(p, axis=-1, keepdims=True)
        o = jnp.dot(p.astype(BF16), mv_ref[0, :, cols].astype(BF16), preferred_element_type=F32)
        outs.append((o / l).astype(BF16))
    o_all = jnp.concatenate(outs, axis=-1)
    o_ref[0] = x + jnp.dot(o_all, wo_ref[...], preferred_element_type=F32)


def _xattn(x, a, wa, layer_a, g, wq, mk, mv, wo, layer, heads, *, name, tm_pref=TILES["xattn_rows"]):
    b, t, d = x.shape
    k = a.shape[2]
    nm, mw = mk.shape[1], mk.shape[2]
    tm = _pick_tile(t, tm_pref, SUBLANES)
    x_spec = pl.BlockSpec((1, tm, d), lambda bi, ti: (bi, ti, 0))
    mem_spec = pl.BlockSpec((1, nm, mw), lambda bi, ti: (bi, 0, 0))
    once = pl.Buffered(1)
    est = 4 * tm * d * 4 + 2 * tm * k * 2 + (k * d + 2 * d * mw) * 2 + 4 * nm * mw * 4 + 3 * tm * d * 4
    return pl.pallas_call(
        functools.partial(_xattn_kernel, heads=heads),
        grid=(b, t // tm),
        in_specs=[x_spec, pl.BlockSpec((1, tm, k), lambda bi, ti: (bi, ti, 0)),
                  pl.BlockSpec((None, k, d), lambda bi, ti: (layer_a, 0, 0), pipeline_mode=once),
                  pl.BlockSpec((1, d), lambda bi, ti: (0, 0)),
                  pl.BlockSpec((None, d, mw), lambda bi, ti: (layer, 0, 0), pipeline_mode=once), mem_spec, mem_spec,
                  pl.BlockSpec((None, mw, d), lambda bi, ti: (layer, 0, 0), pipeline_mode=once)],
        out_specs=x_spec,
        out_shape=jax.ShapeDtypeStruct((b, t, d), F32),
        compiler_params=_params(("parallel", "parallel"), est),
        name=name,
    )(x, a, wa, g.reshape(1, d), wq, mk, mv, wo)


def _ffn_kernel(x_ref, g_ref, wg_ref, wu_ref, wo_ref, gf_ref, o_ref, h_ref, *, final_norm):
    j = pl.program_id(1)

    def mix(h):
        gate = jnp.dot(h, wg_ref[...], preferred_element_type=F32)
        up = jnp.dot(h, wu_ref[...], preferred_element_type=F32)
        act = (jax.nn.silu(gate) * up).astype(BF16)
        return jnp.dot(act, wo_ref[...], preferred_element_type=F32)

    @pl.when(j == 0)
    def _():
        for rows in _row_chunks(x_ref.shape[0]):
            x = x_ref[rows, :]
            h = _rms(x, g_ref[...]).astype(BF16)
            h_ref[rows, :] = h
            o_ref[rows, :] = x + mix(h)

    @pl.when(j > 0)
    def _():
        o_ref[...] += mix(h_ref[...])

    if final_norm:
        @pl.when(j == pl.num_programs(1) - 1)
        def _():
            o_ref[...] = _rms(o_ref[...], gf_ref[...])


def _ffn(x, g, w_in, w_out, layer, g_final, *, final_norm, name, tm_pref=TILES["ffn_rows"], tf_pref=TILES["ffn_cols"]):
    m, d = x.shape
    f = w_out.shape[1]
    tm = _pick_tile(m, tm_pref, SUBLANES)
    tf = _pick_tile(f, tf_pref, LANES)
    nf = f // tf
    x_spec = pl.BlockSpec((tm, d), lambda i, j: (i, 0))
    vec_spec = pl.BlockSpec((1, d), lambda i, j: (0, 0))
    est = 4 * tm * d * 4 + tm * d * 2 + 2 * 3 * d * tf * 2 + 4 * tm * tf * 4
    return pl.pallas_call(
        functools.partial(_ffn_kernel, final_norm=final_norm),
        grid=(m // tm, nf),
        in_specs=[x_spec, vec_spec,
                  pl.BlockSpec((None, d, tf), lambda i, j: (layer, 0, j)),
                  pl.BlockSpec((None, d, tf), lambda i, j: (layer, 0, nf + j)),
                  pl.BlockSpec((None, tf, d), lambda i, j: (layer, j, 0)), vec_spec],
        out_specs=x_spec,
        out_shape=jax.ShapeDtypeStruct((m, d), F32),
        scratch_shapes=[pltpu.VMEM((tm, d), BF16)],
        compiler_params=_params(("parallel", "arbitrary"), est),
        name=name,
    )(x, g.reshape(1, d), w_in, w_in, w_out, g_final.reshape(1, d))


def kernel(x_prompt, x_sample, mem_prompt, cache_fox_k, cache_fox_v, cache_fox_logf, cache_mem_k, cache_mem_v, state_lru_h, state_lru_conv, norm_mix, norm_mem, norm_xattn, norm_ffn, norm_final, fox_w_in, fox_b_f, fox_w_out, lru_w_in, lru_conv_w, lru_conv_b, lru_w_ga, lru_b_a, lru_w_gx, lru_b_x, lru_lambda, lru_w_out, xattn_w_q, xattn_w_kv, xattn_w_o, ffn_w_in, ffn_w_out):
    b, s_len, d = x_prompt.shape
    bd, t_dec, _ = x_sample.shape
    depth = norm_mix.shape[0]
    heads = fox_b_f.shape[1]
    past = cache_fox_k.shape[2]
    n_mem, mem_heads = cache_mem_k.shape[2], cache_mem_k.shape[3]
    mem_w = mem_heads * cache_mem_k.shape[4]
    taps = lru_conv_w.shape[1]
    assert taps - 1 <= SUBLANES <= min(s_len, t_dec) and heads <= LANES

    xp = x_prompt.reshape(b * s_len, d)
    xs = x_sample.reshape(bd * t_dec, d)
    mem = mem_prompt.reshape(b * n_mem, d)
    pk, pv, plf, pmk, pmv, ph, pc = [], [], [], [], [], [], []
    sk, sv, slf, sh, sc = [], [], [], [], []

    fox_in, fox_out = fox_w_in.astype(BF16), fox_w_out.astype(BF16)
    lru_in, lru_out = lru_w_in.astype(BF16), lru_w_out.astype(BF16)
    lru_ga, lru_gx = lru_w_ga.astype(BF16), lru_w_gx.astype(BF16)
    xq, xkv, xo = xattn_w_q.astype(BF16), xattn_w_kv.astype(BF16), xattn_w_o.astype(BF16)
    ffn_in, ffn_out = ffn_w_in.astype(BF16), ffn_w_out.astype(BF16)

    for i in range(depth):
        j = i // 2
        if i % 2 == 0:
            w_f = jnp.pad(fox_in[j, :, 3 * d:], ((0, 0), (0, LANES - heads)))
            b_f = jnp.pad(fox_b_f[j], (0, LANES - heads)).reshape(1, LANES)
            q_scale = (d // heads) ** -0.5 * LOG2E
            proj = functools.partial(_norm_proj, g=norm_mix[i], w=fox_in, layer=j, ns=d, dtypes=[BF16, F32, F32],
                                     acts=[lambda q: q * q_scale, _identity, _identity],
                                     extra=(w_f, b_f, LANES, _log_sigmoid))

            qp, kp, vp, lfp = proj(xp, name=f"fox_proj_p{i}")
            c_p = _cumsum_rows(lfp.reshape(b, s_len, LANES), name=f"fox_cumsum_p{i}")
            attn = _fox_attn(qp.reshape(b, s_len, d), kp.reshape(b, s_len, d), vp.reshape(b, s_len, d), c_p, heads,
                             name=f"fox_attn_p{i}")
            mixed_p, mixed_s, w_mix = attn, None, fox_out
            pk.append(kp.reshape(b, s_len, heads, d // heads))
            pv.append(vp.reshape(b, s_len, heads, d // heads))
            plf.append(lfp[:, :heads].reshape(b, s_len, heads))

            qs, ks, vs, lfs = proj(xs, name=f"fox_proj_s{i}")
            lfs = lfs[:, :heads]
            lf_all = jnp.concatenate([cache_fox_logf[j], lfs.reshape(bd, t_dec, heads)], axis=1)
            total = past + t_dec
            padded = -(-total // LANES) * LANES
            lf_t = jnp.pad(lf_all.transpose(0, 2, 1).reshape(bd * heads, total), ((0, 0), (0, padded - total)))
            c_s = _cumsum_lanes(lf_t, name=f"fox_cumsum_s{i}")
            c_s = c_s.reshape(bd, heads, padded).transpose(0, 2, 1)
            rows = t_dec * heads
            attn = _fox_dec_attn(qs.reshape(bd, rows, d // heads), ks.reshape(bd, rows, d // heads),
                                 vs.reshape(bd, rows, d // heads), cache_fox_k, cache_fox_v, j,
                                 c_s[:, :past].reshape(bd, 1, past * heads), c_s[:, past:total].reshape(bd, 1, rows),
                                 name=f"fox_attn_s{i}")
            mixed_s = attn.reshape(bd, t_dec, d)
            sk.append(ks.reshape(bd, t_dec, heads, d // heads))
            sv.append(vs.reshape(bd, t_dec, heads, d // heads))
            slf.append(lfs.reshape(bd, t_dec, heads))
        else:
            width = lru_in.shape[2] // 2
            proj = functools.partial(_norm_proj, g=norm_mix[i], w=lru_in, layer=j, ns=width,
                                     acts=[jax.nn.gelu, _identity], dtypes=[BF16, F32], tn_pref=TILES["lru_proj_cols"])
            scan = functools.partial(_lru_scan, conv_w=lru_conv_w[j], conv_b=lru_conv_b[j], w_ga=lru_ga, b_a=lru_b_a[j],
                                     w_gx=lru_gx, b_x=lru_b_x[j], lam=lru_lambda[j], layer=j)

            gate, u = proj(xp, name=f"lru_proj_p{i}")
            y, h8, u8 = scan(u.reshape(b, s_len, width), gate.reshape(b, s_len, width),
                             h0=jnp.zeros((b, 1, width), F32), buf0=jnp.zeros((b, SUBLANES, width), F32),
                             name=f"lru_scan_p{i}")
            mixed_p, w_mix = y, lru_out
            ph.append(h8[:, SUBLANES - 1])
            pc.append(u8[:, SUBLANES - (taps - 1):])

            gate, u = proj(xs, name=f"lru_proj_s{i}")
            buf0 = jnp.pad(state_lru_conv[j], ((0, 0), (SUBLANES - (taps - 1), 0), (0, 0)))
            y, h8, u8 = scan(u.reshape(bd, t_dec, width), gate.reshape(bd, t_dec, width),
                             h0=state_lru_h[j].reshape(bd, 1, width), buf0=buf0, name=f"lru_scan_s{i}")
            mixed_s = y
            sh.append(h8[:, SUBLANES - 1])
            sc.append(u8[:, SUBLANES - (taps - 1):])

        mk, mv = _norm_proj(mem, norm_mem[i], xkv, i, mem_w, [_identity] * 2, [F32, F32], name=f"mem_kv{i}")
        pmk.append(mk.reshape(b, n_mem, mem_heads, mem_w // mem_heads))
        pmv.append(mv.reshape(b, n_mem, mem_heads, mem_w // mem_heads))
        xp = _xattn(xp.reshape(b, s_len, d), mixed_p, w_mix, j, norm_xattn[i], xq, mk.reshape(b, n_mem, mem_w),
                    mv.reshape(b, n_mem, mem_w), xo, i, mem_heads, name=f"xattn_p{i}").reshape(b * s_len, d)
        xs = _xattn(xs.reshape(bd, t_dec, d), mixed_s, w_mix, j, norm_xattn[i], xq, cache_mem_k[i].reshape(bd, n_mem, mem_w),
                    cache_mem_v[i].reshape(bd, n_mem, mem_w), xo, i, mem_heads, name=f"xattn_s{i}").reshape(bd * t_dec, d)

        last = i == depth - 1
        xp = _ffn(xp, norm_ffn[i], ffn_in, ffn_out, i, norm_final, final_norm=last, name=f"ffn_p{i}")
        xs = _ffn(xs, norm_ffn[i], ffn_in, ffn_out, i, norm_final, final_norm=last, name=f"ffn_s{i}")

    return (xp.reshape(b, s_len, d), xs.reshape(bd, t_dec, d),
            jnp.stack(pk), jnp.stack(pv), jnp.stack(plf), jnp.stack(pmk), jnp.stack(pmv),
            jnp.stack(ph), jnp.stack(pc),
            jnp.stack(sk), jnp.stack(sv), jnp.stack(slf), jnp.stack(sh), jnp.stack(sc))
```
